```python
import jax, jax.numpy as jnp
from jax import lax
import numpy as np


D_MODEL = 1024
BATCH = 4
SEQ = 4096
DEPTH = 4
DEC_BATCH = 128
DEC_SEQ = 4
PAST_LEN = 2048
PAGE_SIZE = 128

HEAD_DIM = 64
NSA_HEADS = 8
NSA_KV_HEADS = 2
NSA_GROUP = NSA_HEADS // NSA_KV_HEADS
CMP_LEN = 32
CMP_STRIDE = 16
SEL_BLOCK = 64
SEL_TOP = 16
WINDOW = 512
SB_HEADS = 8
D_NSA = NSA_HEADS * HEAD_DIM
D_SB = SB_HEADS * HEAD_DIM
D_MIX = D_NSA + D_SB
D_FF = -(-8 * D_MODEL // (3 * 256)) * 256
KV_W = 2 * NSA_KV_HEADS * HEAD_DIM
SPLITS = (D_NSA, KV_W, KV_W, KV_W, 3 * NSA_HEADS, D_SB, 2 * D_SB, D_MIX)
D_IN = D_NSA + 3 * KV_W + 3 * NSA_HEADS + 3 * D_SB + D_MIX
QUERY_BLOCK = 128
FORCE_BONUS = 1.0e4
NEG = -1.0e30
EPS = 1e-6

kernel_name = "nsa_stickbreaking_hybrid_step"


def rms_norm(x, g):
    xf = x.astype(jnp.float32)
    y = xf * lax.rsqrt(jnp.mean(xf * xf, axis=-1, keepdims=True) + EPS)
    return (y * g.astype(jnp.float32)).astype(x.dtype)


def alibi_slopes(n):
    return 2.0 ** (-8.0 * (jnp.arange(n, dtype=jnp.float32) + 1.0) / n)


def masked_softmax(s, mask, axes):
    s = jnp.where(mask, s, NEG)
    p = jnp.where(mask, jnp.exp(s - jnp.max(s, axis=axes, keepdims=True)), 0.0)
    return p / jnp.maximum(jnp.sum(p, axis=axes, keepdims=True), 1e-30)


def project(h, w_in_l):
    B, T, _ = h.shape
    u = h @ w_in_l
    parts = []
    off = 0
    for n in SPLITS:
        parts.append(u[..., off:off + n])
        off += n
    q_nsa, kv_cmp, kv_slc, kv_win, g_nsa, q_sb, kv_sb, g_mix = parts
    q_nsa = q_nsa.reshape(B, T, NSA_HEADS, HEAD_DIM)
    kv_cmp = kv_cmp.reshape(B, T, 2, NSA_KV_HEADS, HEAD_DIM)
    kv_slc = kv_slc.reshape(B, T, 2, NSA_KV_HEADS, HEAD_DIM)
    kv_win = kv_win.reshape(B, T, 2, NSA_KV_HEADS, HEAD_DIM)
    g_nsa = jax.nn.sigmoid(g_nsa.reshape(B, T, NSA_HEADS, 3))
    q_sb = q_sb.reshape(B, T, SB_HEADS, HEAD_DIM)
    kv_sb = kv_sb.reshape(B, T, 2, SB_HEADS, HEAD_DIM)
    g_mix = jax.nn.sigmoid(g_mix)
    return q_nsa, kv_cmp, kv_slc, kv_win, g_nsa, q_sb, kv_sb, g_mix


def compress(k, pe, w):
    B, T = k.shape[:2]
    n_chunks = T // CMP_STRIDE
    ch = k[:, :n_chunks * CMP_STRIDE].reshape(B, n_chunks, CMP_STRIDE, NSA_KV_HEADS, HEAD_DIM)
    lo = jnp.einsum('bcpgd,pde->bcge', ch + pe[:CMP_STRIDE][:, None, :], w[:CMP_STRIDE])
    hi = jnp.einsum('bcpgd,pde->bcge', ch + pe[CMP_STRIDE:][:, None, :], w[CMP_STRIDE:])
    return lo[:, :-1] + hi[:, 1:]


def compress_kv(kv, pe_l, w_l):
    return compress(kv[:, :, 0], pe_l[0], w_l[0]), compress(kv[:, :, 1], pe_l[1], w_l[1])


def to_sel_blocks(kv):
    B, T = kv.shape[:2]
    n_sel = -(-T // SEL_BLOCK)
    kv = jnp.pad(kv, ((0, 0), (0, n_sel * SEL_BLOCK - T), (0, 0), (0, 0), (0, 0)))
    kv = kv.reshape(B, n_sel, SEL_BLOCK, 2, NSA_KV_HEADS, HEAD_DIM).transpose(3, 0, 4, 1, 2, 5)
    return kv[0], kv[1]


def nsa_attend(q, gates, q_pos, kc, vc, ks, vs, kw, vw, kw_pos):
    B, Q = q.shape[:2]
    G, R = NSA_KV_HEADS, NSA_GROUP
    qg = q.reshape(B, Q, G, R, HEAD_DIM) * (HEAD_DIM ** -0.5)
    slopes = alibi_slopes(NSA_HEADS).reshape(G, R)
    n_cmp = kc.shape[1]
    c_end = jnp.arange(n_cmp) * CMP_STRIDE + (CMP_LEN - 1)
    dist_c = (q_pos[:, None] - c_end[None, :]).astype(jnp.float32)
    s_c = jnp.einsum('bqgrd,bcgd->bgrqc', qg, kc).astype(jnp.float32) - slopes[:, :, None, None] * dist_c
    p_c = masked_softmax(s_c, dist_c >= 0, -1)
    o_c = jnp.einsum('bgrqc,bcgd->bqgrd', p_c.astype(vc.dtype), vc)
    n_sel = ks.shape[2]
    ratio = SEL_BLOCK // CMP_STRIDE
    offs = jnp.arange(-1, ratio)
    w_off = jnp.where((offs == -1) | (offs == ratio - 1), 1.0, 2.0)
    cidx = ratio * jnp.arange(n_sel)[:, None] + offs[None, :]
    cw = jnp.where((cidx >= 0) & (cidx < n_cmp), w_off[None, :], 0.0)
    imp = jnp.sum(jnp.take(p_c.sum(axis=2), jnp.clip(cidx, 0, n_cmp - 1), axis=-1) * cw, axis=-1)
    blk = jnp.arange(n_sel)[None, :]
    cur = (q_pos // SEL_BLOCK)[:, None]
    allowed = blk * SEL_BLOCK <= q_pos[:, None]
    forced = (blk == 0) | (blk == cur) | (blk == cur - 1)
    score = jnp.where(allowed, imp + jnp.where(forced, FORCE_BONUS, 0.0), -FORCE_BONUS)
    n_top = min(SEL_TOP, n_sel)
    top_score, top_idx = lax.top_k(score, n_top)
    sel_valid = top_score > -0.5 * FORCE_BONUS
    take = jax.vmap(jax.vmap(lambda blocks, i: blocks[i]))
    flat = top_idx.reshape(B, G, Q * n_top)
    k_sel = take(ks, flat).reshape(B, G, Q, n_top, SEL_BLOCK, HEAD_DIM)
    v_sel = take(vs, flat).reshape(B, G, Q, n_top, SEL_BLOCK, HEAD_DIM)
    s_pos = top_idx[..., None] * SEL_BLOCK + jnp.arange(SEL_BLOCK)
    dist_s = (q_pos[None, None, :, None, None] - s_pos).astype(jnp.float32)
    mask_s = (dist_s >= 0) & sel_valid[..., None]
    s_s = (jnp.einsum('bqgrd,bgqkpd->bgrqkp', qg, k_sel).astype(jnp.float32)
           - slopes[None, :, :, None, None, None] * dist_s[:, :, None])
    p_s = masked_softmax(s_s, mask_s[:, :, None], (-2, -1))
    o_s = jnp.einsum('bgrqkp,bgqkpd->bqgrd', p_s.astype(v_sel.dtype), v_sel)
    dist_w = q_pos[:, None] - kw_pos[None, :]
    mask_w = (dist_w >= 0) & (dist_w < WINDOW) & (kw_pos[None, :] >= 0)
    s_w = (jnp.einsum('bqgrd,bkgd->bgrqk', qg, kw).astype(jnp.float32)
           - slopes[:, :, None, None] * dist_w.astype(jnp.float32))
    p_w = masked_softmax(s_w, mask_w, -1)
    o_w = jnp.einsum('bgrqk,bkgd->bqgrd', p_w.astype(vw.dtype), vw)
    g = gates.reshape(B, Q, G, R, 3)
    o = g[..., 0:1] * o_c + g[..., 1:2] * o_s + g[..., 2:3] * o_w
    return o.reshape(B, Q, D_NSA)


def stick_breaking_attend(q, q_pos, k, v):
    B, Q = q.shape[:2]
    Tk = k.shape[1]
    z = jnp.einsum('bqhd,bkhd->bhqk', q, k).astype(jnp.float32) * (HEAD_DIM ** -0.5)
    mask = jnp.arange(Tk)[None, :] < q_pos[:, None]
    log_keep = jnp.where(mask, jax.nn.log_sigmoid(-z), 0.0)
    later = lax.cumsum(log_keep, axis=3, reverse=True) - log_keep
    a = jnp.where(mask, jnp.exp(jax.nn.log_sigmoid(z) + later), 0.0)
    return jnp.einsum('bhqk,bkhd->bqhd', a.astype(v.dtype), v).reshape(B, Q, D_SB)


def mix_prompt(h, w_in_l, pe_l, w_cmp_l, w_out_l):
    B, T, _ = h.shape
    q_nsa, kv_cmp, kv_slc, kv_win, g_nsa, q_sb, kv_sb, g_mix = project(h, w_in_l)
    kc, vc = compress_kv(kv_cmp, pe_l, w_cmp_l)
    ks, vs = to_sel_blocks(kv_slc)
    kw_pad = jnp.pad(kv_win, ((0, 0), (WINDOW, 0), (0, 0), (0, 0), (0, 0)))
    k_sb, v_sb = kv_sb[:, :, 0], kv_sb[:, :, 1]

    def one_block(i):
        q0 = i * QUERY_BLOCK
        q_pos = q0 + jnp.arange(QUERY_BLOCK)
        qb = lax.dynamic_slice_in_dim(q_nsa, q0, QUERY_BLOCK, axis=1)
        gb = lax.dynamic_slice_in_dim(g_nsa, q0, QUERY_BLOCK, axis=1)
        kvw = lax.dynamic_slice_in_dim(kw_pad, q0, WINDOW + QUERY_BLOCK, axis=1)
        kw_pos = q0 - WINDOW + jnp.arange(WINDOW + QUERY_BLOCK)
        o_a = nsa_attend(qb, gb, q_pos, kc, vc, ks, vs, kvw[:, :, 0], kvw[:, :, 1], kw_pos)
        o_b = stick_breaking_attend(lax.dynamic_slice_in_dim(q_sb, q0, QUERY_BLOCK, axis=1), q_pos, k_sb, v_sb)
        return jnp.concatenate([o_a, o_b], axis=-1)

    o = lax.map(one_block, jnp.arange(T // QUERY_BLOCK))
    o = o.transpose(1, 0, 2, 3).reshape(B, T, D_MIX)
    y = (g_mix * o) @ w_out_l
    return y, kv_cmp, kv_slc, kv_win[:, T - min(WINDOW, T):], kv_sb


def mix_sample(h, page_table, c_cmp, c_slc, c_win, c_sb, w_in_l, pe_l, w_cmp_l, w_out_l):
    B, S, _ = h.shape
    past_len = page_table.shape[1] * PAGE_SIZE
    q_nsa, kv_cmp, kv_slc, kv_win, g_nsa, q_sb, kv_sb, g_mix = project(h, w_in_l)

    def with_past(cache, new):
        past = cache[page_table].reshape((B, past_len) + cache.shape[2:])
        return jnp.concatenate([past, new], axis=1)

    q_pos = past_len + jnp.arange(S)
    kc, vc = compress_kv(with_past(c_cmp, kv_cmp), pe_l, w_cmp_l)
    ks, vs = to_sel_blocks(with_past(c_slc, kv_slc))
    kvw = jnp.concatenate([c_win, kv_win], axis=1)
    kw_pos = past_len - c_win.shape[1] + jnp.arange(kvw.shape[1])
    o_a = nsa_attend(q_nsa, g_nsa, q_pos, kc, vc, ks, vs, kvw[:, :, 0], kvw[:, :, 1], kw_pos)
    kv_sb_all = with_past(c_sb, kv_sb)
    o_b = stick_breaking_attend(q_sb, q_pos, kv_sb_all[:, :, 0], kv_sb_all[:, :, 1])
    o = jnp.concatenate([o_a, o_b], axis=-1)
    y = (g_mix * o) @ w_out_l
    return y, kv_cmp, kv_slc, kvw[:, kvw.shape[1] - c_win.shape[1]:], kv_sb


def ffn_block(x, g_pre, g_post, w1, w2):
    h = rms_norm(x, g_pre)
    a, b = jnp.split(h @ w1, 2, axis=-1)
    return x + rms_norm((jax.nn.silu(a) * b) @ w2, g_post)


def setup_inputs(seed: int = 0) -> dict:
    key = jax.random.key(seed)
    ks = jax.random.split(key, 20)
    f32 = jnp.float32
    n_pages = PAST_LEN // PAGE_SIZE
    n_used = DEC_BATCH * n_pages
    n_pool = n_used + (n_used + 3) // 4
    win_buf = min(WINDOW, PAST_LEN)

    def nrm(k, shape, s=1.0):
        return jax.random.normal(k, shape, f32) * s

    page_table = jax.random.permutation(ks[6], n_pool)[:n_used].reshape(DEC_BATCH, n_pages).astype(jnp.int32)
    return {
        "x_prompt": nrm(ks[0], (BATCH, SEQ, D_MODEL)),
        "x_sample": nrm(ks[1], (DEC_BATCH, DEC_SEQ, D_MODEL)),
        "cache_cmp_kv": nrm(ks[2], (DEPTH, n_pool, PAGE_SIZE, 2, NSA_KV_HEADS, HEAD_DIM)),
        "cache_slc_kv": nrm(ks[3], (DEPTH, n_pool, PAGE_SIZE, 2, NSA_KV_HEADS, HEAD_DIM)),
        "cache_win_kv": nrm(ks[4], (DEPTH, DEC_BATCH, win_buf, 2, NSA_KV_HEADS, HEAD_DIM)),
        "cache_sb_kv": nrm(ks[5], (DEPTH, n_pool, PAGE_SIZE, 2, SB_HEADS, HEAD_DIM)),
        "page_table": page_table,
        "g_pre_mix": 1.0 + nrm(ks[7], (DEPTH, D_MODEL), 0.05),
        "g_post_mix": 1.0 + nrm(ks[8], (DEPTH, D_MODEL), 0.05),
        "g_pre_ffn": 1.0 + nrm(ks[9], (DEPTH, D_MODEL), 0.05),
        "g_post_ffn": 1.0 + nrm(ks[10], (DEPTH, D_MODEL), 0.05),
        "w_in": nrm(ks[11], (DEPTH, D_MODEL, D_IN), D_MODEL ** -0.5),
        "pe_cmp": nrm(ks[12], (DEPTH, 2, CMP_LEN, HEAD_DIM), 0.5),
        "w_cmp": nrm(ks[13], (DEPTH, 2, CMP_LEN, HEAD_DIM, HEAD_DIM), (CMP_LEN * HEAD_DIM) ** -0.5),
        "w_out": nrm(ks[14], (DEPTH, D_MIX, D_MODEL), D_MIX ** -0.5),
        "w_ffn_in": nrm(ks[15], (DEPTH, D_MODEL, 2 * D_FF), D_MODEL ** -0.5),
        "w_ffn_out": nrm(ks[16], (DEPTH, D_FF, D_MODEL), D_FF ** -0.5),
    }


def reference(x_prompt, x_sample, cache_cmp_kv, cache_slc_kv, cache_win_kv, cache_sb_kv, page_table,
              g_pre_mix, g_post_mix, g_pre_ffn, g_post_ffn, w_in, pe_cmp, w_cmp, w_out, w_ffn_in, w_ffn_out):
    xp, xs = x_prompt, x_sample
    cmp_p, cmp_s, slc_p, slc_s, win_p, win_s, sb_p, sb_s = [], [], [], [], [], [], [], []
    for l in range(DEPTH):
        yp, a, b, c, d = mix_prompt(rms_norm(xp, g_pre_mix[l]), w_in[l], pe_cmp[l], w_cmp[l], w_out[l])
        xp = xp + rms_norm(yp, g_post_mix[l])
        xp = ffn_block(xp, g_pre_ffn[l], g_post_ffn[l], w_ffn_in[l], w_ffn_out[l])
        cmp_p.append(a); slc_p.append(b); win_p.append(c); sb_p.append(d)
        ys, a, b, c, d = mix_sample(rms_norm(xs, g_pre_mix[l]), page_table, cache_cmp_kv[l], cache_slc_kv[l],
                                    cache_win_kv[l], cache_sb_kv[l], w_in[l], pe_cmp[l], w_cmp[l], w_out[l])
        xs = xs + rms_norm(ys, g_post_mix[l])
        xs = ffn_block(xs, g_pre_ffn[l], g_post_ffn[l], w_ffn_in[l], w_ffn_out[l])
        cmp_s.append(a); slc_s.append(b); win_s.append(c); sb_s.append(d)
    return (xp, xs, jnp.stack(cmp_p), jnp.stack(cmp_s), jnp.stack(slc_p), jnp.stack(slc_s),
            jnp.stack(win_p), jnp.stack(win_s), jnp.stack(sb_p), jnp.stack(sb_s))
```

```python
import functools

import numpy as np
import jax
import jax.numpy as jnp
from jax import lax
from jax.experimental import pallas as pl
from jax.experimental.pallas import tpu as pltpu

F32 = jnp.float32
BF16 = jnp.bfloat16

D_MODEL = 1024
HEAD_DIM = 64
NSA_HEADS = 8
NSA_KV_HEADS = 2
NSA_GROUP = NSA_HEADS // NSA_KV_HEADS
CMP_LEN = 32
CMP_STRIDE = 16
SEL_BLOCK = 64
SEL_TOP = 16
WINDOW = 512
SB_HEADS = 8
D_NSA = NSA_HEADS * HEAD_DIM
D_SB = SB_HEADS * HEAD_DIM
D_MIX = D_NSA + D_SB
D_FF = -(-8 * D_MODEL // (3 * 256)) * 256
KV_W = 2 * NSA_KV_HEADS * HEAD_DIM
PAGE_SIZE = 128
FORCE_BONUS = 1.0e4
NEG = -1.0e30
EPS = 1e-6
SLOPES = tuple(2.0 ** (-8.0 * (i + 1.0) / NSA_HEADS) for i in range(NSA_HEADS))

LANES = 128
S_PAD = 8
QB = 128
SEL_TILE = 512
SB_TQ = 256
SB_TK = 256
ROW_TILE = 256
VMEM_LIMIT = 56 * 1024 * 1024

C_QN = (0, 1024)
C_CMP = (1024, 1280)
C_SLC = (1280, 1536)
C_WIN = (1536, 1792)
C_GN = (1792, 1920)
C_QS = (1920, 2432)
C_SB = (2432, 3456)
C_GM = (3456, 4480)
W_CAT = 4480


def _dot(a, b):
    return jnp.dot(a, b, preferred_element_type=F32)


def _dot_nt(a, b):
    return lax.dot_general(a, b, (((1,), (1,)), ((), ())), preferred_element_type=F32)


def _split3(a):
    a1 = a.astype(BF16)
    r1 = a - a1.astype(F32)
    a2 = r1.astype(BF16)
    a3 = (r1 - a2.astype(F32)).astype(BF16)
    return a1, a2, a3


def _dot3_l(a, b):
    a1, a2, a3 = _split3(a)
    return _dot(a1, b) + _dot(a2, b) + _dot(a3, b)


def _dot3_r(a, b):
    b1, b2, b3 = _split3(b)
    return _dot(a, b1) + _dot(a, b2) + _dot(a, b3)


def _sigmoid(x):
    return 1.0 / (1.0 + jnp.exp(-x))


def _rms(x, g):
    return (x * lax.rsqrt(jnp.mean(x * x, axis=-1, keepdims=True) + EPS)) * g


def _masked_softmax(s, mask):
    s = jnp.where(mask, s, NEG)
    m = jnp.max(s, axis=-1, keepdims=True)
    p = jnp.where(mask, jnp.exp(s - m), 0.0)
    return p / jnp.maximum(jnp.sum(p, axis=-1, keepdims=True), 1e-30)


def _topk_select(score, n_blk):
    n_pad = -(-n_blk // 8) * 8
    st = score.T[:n_pad, :]
    j = lax.broadcasted_iota(jnp.int32, (n_pad, LANES), 0)
    rank = jnp.zeros((n_pad, LANES), F32)
    for i in range(n_blk):
        row = jnp.broadcast_to(st[i:i + 1, :], (n_pad, LANES))
        beats = (row > st) | ((row == st) & (j > i))
        rank = rank + jnp.where(beats, 1.0, 0.0)
    sel_t = jnp.where(rank < float(SEL_TOP), 1.0, 0.0)
    if n_pad < LANES:
        sel_t = jnp.concatenate([sel_t, jnp.zeros((LANES - n_pad, LANES), F32)], axis=0)
    return sel_t.T


def _pair_heads(pieces, g):
    lane = lax.broadcasted_iota(jnp.int32, pieces[0].shape, 1)
    outs = []
    for m in range(2):
        a, b = pieces[2 * m], pieces[2 * m + 1]
        if g == 0:
            left, right = a, pltpu.roll(b, 64, 1)
        else:
            left, right = pltpu.roll(a, 64, 1), b
        outs.append(jnp.where(lane < 64, left, right))
    return jnp.concatenate(outs, axis=1)


def _inproj_kernel(x_ref, g_ref, w_ref, qn_ref, cmp_ref, slc_ref, win_ref, slcb_ref, winb_ref,
                   gn_ref, qs_ref, sb_ref, sbb_ref, gm_ref):
    hb = _rms(x_ref[...], g_ref[...]).astype(BF16)

    def proj(sec):
        return _dot(hb, w_ref[:, sec[0]:sec[1]])

    qn_ref[...] = (proj(C_QN) * (HEAD_DIM ** -0.5)).astype(BF16)
    cmp_ref[...] = proj(C_CMP)
    u = proj(C_SLC)
    slc_ref[...] = u
    slcb_ref[...] = u.astype(BF16)
    u = proj(C_WIN)
    win_ref[...] = u
    winb_ref[...] = u.astype(BF16)
    gn_ref[...] = _sigmoid(proj(C_GN))
    qs_ref[...] = (proj(C_QS) * (HEAD_DIM ** -0.5)).astype(BF16)
    u = proj(C_SB)
    sb_ref[...] = u
    sbb_ref[...] = u.astype(BF16)
    gm_ref[...] = _sigmoid(proj(C_GM))


def _inproj(x2d, g, w_cat):
    n = x2d.shape[0]
    tm = min(ROW_TILE, n)
    widths = [(1024, BF16), (256, F32), (256, F32), (256, F32), (256, BF16), (256, BF16),
              (128, F32), (512, BF16), (1024, F32), (1024, BF16), (1024, F32)]
    return pl.pallas_call(
        _inproj_kernel,
        grid=(n // tm,),
        in_specs=[pl.BlockSpec((tm, D_MODEL), lambda i: (i, 0)),
                  pl.BlockSpec((1, D_MODEL), lambda i: (0, 0)),
                  pl.BlockSpec((D_MODEL, W_CAT), lambda i: (0, 0))],
        out_specs=[pl.BlockSpec((tm, w), lambda i: (i, 0)) for w, _ in widths],
        out_shape=[jax.ShapeDtypeStruct((n, w), dt) for w, dt in widths],
        compiler_params=pltpu.CompilerParams(dimension_semantics=("arbitrary",), vmem_limit_bytes=VMEM_LIMIT),
        name="inproj",
    )(x2d, g, w_cat)


def _compress_kernel(*refs, n_pages, n_prefetch):
    refs = refs[n_prefetch:]
    page_refs = refs[:n_pages]
    pe_ref, wlo_ref, whi_ref, out_ref = refs[n_pages:]
    if n_pages == 1:
        x = page_refs[0][0]
    else:
        x = jnp.concatenate([r[0, 0] for r in page_refs], axis=0)
    rows = x.shape[0]
    lo = _dot((x + pe_ref[0:1, :]).astype(BF16), wlo_ref[...])
    hi = _dot((x + pe_ref[1:2, :]).astype(BF16), whi_ref[...])
    out = lo + pltpu.roll(hi, rows - 1, 0)
    c = lax.broadcasted_iota(jnp.int32, out.shape, 0)
    out_ref[0] = jnp.where(c < rows - 1, out, 0.0)


def _compress_prompt(kv_cmp, pe2, wlo, whi):
    b, t, _ = kv_cmp.shape
    rows = t // CMP_STRIDE
    x = kv_cmp.reshape(b, rows, CMP_STRIDE * KV_W)
    return pl.pallas_call(
        functools.partial(_compress_kernel, n_pages=1, n_prefetch=0),
        grid=(b,),
        in_specs=[pl.BlockSpec((1, rows, CMP_STRIDE * KV_W), lambda i: (i, 0, 0)),
                  pl.BlockSpec((2, CMP_STRIDE * KV_W), lambda i: (0, 0)),
                  pl.BlockSpec((CMP_STRIDE * KV_W, KV_W), lambda i: (0, 0)),
                  pl.BlockSpec((CMP_STRIDE * KV_W, KV_W), lambda i: (0, 0))],
        out_specs=pl.BlockSpec((1, rows, KV_W), lambda i: (i, 0, 0)),
        out_shape=jax.ShapeDtypeStruct((b, rows, KV_W), F32),
        compiler_params=pltpu.CompilerParams(dimension_semantics=("arbitrary",), vmem_limit_bytes=VMEM_LIMIT),
        name="compress_prompt",
    )(x, pe2, wlo, whi)


def _compress_sample(cache_cmp, page_table, lidx, pe2, wlo, whi):
    depth, n_pool = cache_cmp.shape[:2]
    nb, n_pages = page_table.shape
    cpp = PAGE_SIZE // CMP_STRIDE
    x = cache_cmp.reshape(depth, n_pool, cpp, CMP_STRIDE * KV_W)
    rows = n_pages * cpp

    def page_spec(i):
        return pl.BlockSpec((1, 1, cpp, CMP_STRIDE * KV_W), lambda b, pt, li: (li[0], pt[b, i], 0, 0))

    return pl.pallas_call(
        functools.partial(_compress_kernel, n_pages=n_pages, n_prefetch=2),
        grid_spec=pltpu.PrefetchScalarGridSpec(
            num_scalar_prefetch=2,
            grid=(nb,),
            in_specs=[page_spec(i) for i in range(n_pages)] + [
                pl.BlockSpec((2, CMP_STRIDE * KV_W), lambda b, pt, li: (0, 0)),
                pl.BlockSpec((CMP_STRIDE * KV_W, KV_W), lambda b, pt, li: (0, 0)),
                pl.BlockSpec((CMP_STRIDE * KV_W, KV_W), lambda b, pt, li: (0, 0))],
            out_specs=pl.BlockSpec((1, rows, KV_W), lambda b, pt, li: (b, 0, 0))),
        out_shape=jax.ShapeDtypeStruct((nb, rows, KV_W), F32),
        compiler_params=pltpu.CompilerParams(dimension_semantics=("arbitrary",), vmem_limit_bytes=VMEM_LIMIT),
        name="compress_sample",
    )(page_table, lidx, *([x] * n_pages), pe2, wlo, whi)


def _nsa_prompt_kernel(q_ref, gate_ref, kc_ref, slc_ref, win_ref, m_ref, o_ref, m_scr, l_scr, acc_scr, *, n_sel):
    qb = pl.program_id(1)
    q0 = qb * QB
    q_all = q_ref[0]
    gates = gate_ref[0]
    kc = kc_ref[0]
    n_cmp_pad = kc.shape[0]
    kc2 = kc[:, :LANES].astype(BF16)
    vc2 = kc[:, LANES:].astype(BF16)
    qpos = q0 + lax.broadcasted_iota(jnp.int32, (QB, 1), 0)
    c_end = lax.broadcasted_iota(jnp.int32, (1, n_cmp_pad), 1) * CMP_STRIDE + (CMP_LEN - 1)
    dist_c = (qpos - c_end).astype(F32)
    mask_c = dist_c >= 0.0
    blk = lax.broadcasted_iota(jnp.int32, (QB, LANES), 1)
    cur = qpos >> 6
    allowed = (blk * SEL_BLOCK <= qpos) & (blk < n_sel)
    forced = (blk == 0) | (blk == cur) | (blk == cur - 1)
    w_start = jnp.maximum(q0 - WINDOW, 0)
    n_win = WINDOW + QB
    n_tiles = (q0 + QB + SEL_TILE - 1) // SEL_TILE

    for g in range(NSA_KV_HEADS):
        heads = [g * NSA_GROUP + r for r in range(NSA_GROUP)]
        qm = jnp.concatenate([q_all[:, h * LANES:(h + 1) * LANES] for h in heads], axis=0)

        s = _dot_nt(qm, kc2)
        p_sum = jnp.zeros((QB, n_cmp_pad), F32)
        p_rows = []
        for r, h in enumerate(heads):
            p = _masked_softmax(s[r * QB:(r + 1) * QB] - SLOPES[h] * dist_c, mask_c)
            p_sum = p_sum + p
            p_rows.append(p.astype(BF16))
        o_c = _dot(jnp.concatenate(p_rows, axis=0), vc2)

        imp = _dot3_l(p_sum, m_ref[...])
        score = jnp.where(allowed, imp + jnp.where(forced, FORCE_BONUS, 0.0), -FORCE_BONUS)
        score = jnp.where(blk < n_sel, score, -3.0 * FORCE_BONUS)
        sel = jnp.where(allowed, _topk_select(score, n_sel), 0.0).astype(BF16)

        m_scr[...] = jnp.full(m_scr.shape, NEG, F32)
        l_scr[...] = jnp.zeros(l_scr.shape, F32)
        acc_scr[...] = jnp.zeros(acc_scr.shape, F32)

        def sel_tile(kt, carry, qm=qm, sel=sel, heads=heads):
            k0 = pl.multiple_of(kt * SEL_TILE, SEL_TILE)
            k2 = slc_ref[0, pl.ds(k0, SEL_TILE), 0:LANES]
            v2 = slc_ref[0, pl.ds(k0, SEL_TILE), LANES:2 * LANES]
            s = _dot_nt(qm, k2)
            kpos = k0 + lax.broadcasted_iota(jnp.int32, (1, SEL_TILE), 1)
            dist = (qpos - kpos).astype(F32)
            e_row = lax.broadcasted_iota(jnp.int32, (LANES, SEL_TILE), 0)
            e_col = lax.broadcasted_iota(jnp.int32, (LANES, SEL_TILE), 1)
            expand = jnp.where(e_row == kt * (SEL_TILE // SEL_BLOCK) + (e_col >> 6), 1.0, 0.0).astype(BF16)
            mask = (_dot(sel, expand) > 0.5) & (dist >= 0.0)
            p_rows = []
            for r, h in enumerate(heads):
                rows = slice(r * QB, (r + 1) * QB)
                sr = jnp.where(mask, s[rows] - SLOPES[h] * dist, NEG)
                m_old = m_scr[rows]
                m_new = jnp.maximum(m_old, jnp.max(sr, axis=-1, keepdims=True))
                alpha = jnp.exp(m_old - m_new)
                p = jnp.where(mask, jnp.exp(sr - m_new), 0.0)
                l_scr[rows] = alpha * l_scr[rows] + jnp.sum(p, axis=-1, keepdims=True)
                m_scr[rows] = m_new
                acc_scr[rows] = alpha * acc_scr[rows]
                p_rows.append(p.astype(BF16))
            acc_scr[...] += _dot(jnp.concatenate(p_rows, axis=0), v2)
            return carry

        lax.fori_loop(0, n_tiles, sel_tile, 0)
        o_s = acc_scr[...] / jnp.maximum(l_scr[...], 1e-30)

        kw = win_ref[0, pl.ds(pl.multiple_of(w_start, QB), n_win), :]
        s = _dot_nt(qm, kw[:, :LANES])
        kpos = w_start + lax.broadcasted_iota(jnp.int32, (1, n_win), 1)
        dist_i = qpos - kpos
        mask_w = (dist_i >= 0) & (dist_i < WINDOW)
        dist_w = dist_i.astype(F32)
        p_rows = []
        for r, h in enumerate(heads):
            p_rows.append(_masked_softmax(s[r * QB:(r + 1) * QB] - SLOPES[h] * dist_w, mask_w).astype(BF16))
        o_w = _dot(jnp.concatenate(p_rows, axis=0), kw[:, LANES:])

        pieces = []
        for r, h in enumerate(heads):
            rows = slice(r * QB, (r + 1) * QB)
            pieces.append(gates[:, 3 * h:3 * h + 1] * o_c[rows] + gates[:, 3 * h + 1:3 * h + 2] * o_s[rows]
                          + gates[:, 3 * h + 2:3 * h + 3] * o_w[rows])
        o_ref[0, :, g * 256:(g + 1) * 256] = _pair_heads(pieces, g)


def _sel_weights(n_cmp_pad, n_cmp, n_sel):
    ratio = SEL_BLOCK // CMP_STRIDE
    m = np.zeros((n_cmp_pad, LANES), np.float32)
    for j in range(n_sel):
        for off in range(-1, ratio):
            c = ratio * j + off
            if 0 <= c < n_cmp:
                m[c, j] += 1.0 if off in (-1, ratio - 1) else 2.0
    return jnp.asarray(m, BF16)


def _nsa_prompt(qn, gn, kc, slcb, winb):
    b, t, _ = qn.shape
    n_sel = -(-t // SEL_BLOCK)
    n_cmp_pad = kc.shape[1]
    m_mat = _sel_weights(n_cmp_pad, t // CMP_STRIDE - 1, n_sel)
    return pl.pallas_call(
        functools.partial(_nsa_prompt_kernel, n_sel=n_sel),
        grid=(b, t // QB),
        in_specs=[pl.BlockSpec((1, QB, NSA_HEADS * LANES), lambda i, j: (i, j, 0)),
                  pl.BlockSpec((1, QB, LANES), lambda i, j: (i, j, 0)),
                  pl.BlockSpec((1, n_cmp_pad, KV_W), lambda i, j: (i, 0, 0)),
                  pl.BlockSpec((1, t, KV_W), lambda i, j: (i, 0, 0)),
                  pl.BlockSpec((1, t, KV_W), lambda i, j: (i, 0, 0)),
                  pl.BlockSpec((n_cmp_pad, LANES), lambda i, j: (0, 0))],
        out_specs=pl.BlockSpec((1, QB, D_NSA), lambda i, j: (i, j, 0)),
        out_shape=jax.ShapeDtypeStruct((b, t, D_NSA), F32),
        scratch_shapes=[pltpu.VMEM((NSA_GROUP * QB, 1), F32), pltpu.VMEM((NSA_GROUP * QB, 1), F32),
                        pltpu.VMEM((NSA_GROUP * QB, LANES), F32)],
        compiler_params=pltpu.CompilerParams(dimension_semantics=("arbitrary", "arbitrary"),
                                             vmem_limit_bytes=VMEM_LIMIT),
        name="nsa_prompt",
    )(qn, gn, kc, slcb, winb, m_mat)


def _softplus(z):
    return jnp.maximum(z, 0.0) + jnp.log1p(jnp.exp(-jnp.abs(z)))


def _suffix_sums(lk, u):
    rows = lk.shape[0]
    hi = lk.astype(BF16)
    lo = (lk - hi.astype(F32)).astype(BF16)
    both = _dot(jnp.concatenate([hi, lo], axis=0), u)
    return both[:rows] + both[rows:]


def _sb_prompt_kernel(q_ref, k_ref, v_ref, u_ref, o_ref, c_scr, acc_scr):
    qb = pl.program_id(2)
    q2 = q_ref[0]
    lane = lax.broadcasted_iota(jnp.int32, q2.shape, 1)
    qms = [jnp.where((lane >> 6) == h, q2, jnp.zeros_like(q2)) for h in range(2)]
    qpos = qb * SB_TQ + lax.broadcasted_iota(jnp.int32, (SB_TQ, 1), 0)
    c_scr[...] = jnp.zeros(c_scr.shape, F32)
    acc_scr[...] = jnp.zeros(acc_scr.shape, F32)
    u = u_ref[...]

    def tile(i, carry):
        kt = qb - i
        k0 = pl.multiple_of(kt * SB_TK, SB_TK)
        k2 = k_ref[0, pl.ds(k0, SB_TK), :]
        v2 = v_ref[0, pl.ds(k0, SB_TK), :]
        kpos = k0 + lax.broadcasted_iota(jnp.int32, (1, SB_TK), 1)
        mask = kpos < qpos
        for h in range(2):
            z = _dot_nt(qms[h], k2)
            lk = jnp.where(mask, -_softplus(z), 0.0)
            later = c_scr[h] + _suffix_sums(lk, u)
            a = jnp.where(mask, jnp.exp(z + lk + later), 0.0)
            acc_scr[h] += _dot(a.astype(BF16), v2)
            c_scr[h] += jnp.sum(lk, axis=-1, keepdims=True)
        return carry

    lax.fori_loop(0, qb + 1, tile, 0)
    lane_o = lax.broadcasted_iota(jnp.int32, (SB_TQ, LANES), 1)
    o_ref[0] = jnp.where(lane_o < HEAD_DIM, acc_scr[0], acc_scr[1])


def _later_matrix(n):
    return jnp.asarray(np.tril(np.ones((n, n), np.float32), -1), BF16)


def _sb_prompt(qs, sbb):
    b, t, _ = qs.shape
    n_hp = SB_HEADS // 2
    return pl.pallas_call(
        _sb_prompt_kernel,
        grid=(b, n_hp, t // SB_TQ),
        in_specs=[pl.BlockSpec((1, SB_TQ, LANES), lambda i, h, j: (i, j, h)),
                  pl.BlockSpec((1, t, LANES), lambda i, h, j: (i, 0, h)),
                  pl.BlockSpec((1, t, LANES), lambda i, h, j: (i, 0, n_hp + h)),
                  pl.BlockSpec((SB_TK, SB_TK), lambda i, h, j: (0, 0))],
        out_specs=pl.BlockSpec((1, SB_TQ, LANES), lambda i, h, j: (i, j, h)),
        out_shape=jax.ShapeDtypeStruct((b, t, D_SB), F32),
        scratch_shapes=[pltpu.VMEM((2, SB_TQ, 1), F32), pltpu.VMEM((2, SB_TQ, LANES), F32)],
        compiler_params=pltpu.CompilerParams(dimension_semantics=("arbitrary", "arbitrary", "arbitrary"),
                                             vmem_limit_bytes=VMEM_LIMIT),
        name="sb_prompt",
    )(qs, sbb, sbb, _later_matrix(SB_TK))


def _pad_rows(x, rows):
    return jnp.concatenate([x, jnp.zeros((rows - x.shape[0], x.shape[1]), x.dtype)], axis=0)


def _sample_kernel(*refs, n_pages, past_len, s_new, n_sel, aliased):
    pt_ref, li_ref = refs[:2]
    refs = refs[2:]
    qn_ref, gate_ref, kc_ref = refs[:3]
    slc_pages = refs[3:3 + n_pages]
    refs = refs[3 + n_pages:]
    slcnew_ref, wcache_ref, wnew_ref, qs_ref = refs[:4]
    sb_pages = refs[4:4 + n_pages]
    refs = refs[4 + n_pages:]
    sbnew_ref, m_ref, slope_ref, gsum_ref, expand_ref, u_ref = refs[:6]
    refs = refs[6:]
    if aliased:
        refs = refs[1:]
    onsa_ref, osb_ref, wout_ref = refs
    del pt_ref, li_ref

    rows = NSA_HEADS * S_PAD
    row = lax.broadcasted_iota(jnp.int32, (rows, 1), 0)
    step = row & (S_PAD - 1)
    qpos = past_len + step
    slope = slope_ref[...]
    gates = gate_ref[0]
    q_all = qn_ref[0]
    qm = jnp.concatenate([q_all[:, h * LANES:(h + 1) * LANES] for h in range(NSA_HEADS)], axis=0)
    lane = lax.broadcasted_iota(jnp.int32, (rows, LANES), 1)

    kc = kc_ref[0]
    n_cmp_pad = kc.shape[0]
    c_end = lax.broadcasted_iota(jnp.int32, (1, n_cmp_pad), 1) * CMP_STRIDE + (CMP_LEN - 1)
    dist_c = (qpos - c_end).astype(F32)
    p_c = _masked_softmax(_dot_nt(qm, kc[:, :LANES].astype(BF16)) - slope * dist_c, dist_c >= 0.0)
    o_c = _dot(p_c.astype(BF16), kc[:, LANES:].astype(BF16))

    p_sum = _dot3_r(gsum_ref[...], p_c)
    imp = _dot3_l(p_sum, m_ref[...])
    cur = qpos >> 6
    allowed = (lane * SEL_BLOCK <= qpos) & (lane < n_sel)
    forced = (lane == 0) | (lane == cur) | (lane == cur - 1)
    score = jnp.where(allowed, imp + jnp.where(forced, FORCE_BONUS, 0.0), -FORCE_BONUS)
    score = jnp.where(lane < n_sel, score, -3.0 * FORCE_BONUS)
    sel = _topk_select(_pad_rows(score, LANES), n_sel)[:rows]
    sel = jnp.where(allowed, sel, 0.0)

    k_past = jnp.concatenate([r[0, 0, :, 0:LANES].astype(BF16) for r in slc_pages], axis=0)
    v_past = jnp.concatenate([r[0, 0, :, LANES:2 * LANES].astype(BF16) for r in slc_pages], axis=0)
    new = _pad_rows(slcnew_ref[0], LANES).astype(BF16)
    s_past = _dot_nt(qm, k_past)
    s_nw = _dot_nt(qm, new[:, :LANES])
    kpos = lax.broadcasted_iota(jnp.int32, (1, past_len), 1)
    dist_p = (qpos - kpos).astype(F32)
    mask_p = (_dot(sel.astype(BF16), expand_ref[...]) > 0.5) & (dist_p >= 0.0)
    dist_n = (step - lane).astype(F32)
    last_blk = past_len // SEL_BLOCK
    sel_last = jnp.sum(jnp.where(lane == last_blk, sel, 0.0), axis=-1, keepdims=True)
    mask_n = (dist_n >= 0.0) & (sel_last > 0.5) & (lane < S_PAD)
    s_past = jnp.where(mask_p, s_past - slope * dist_p, NEG)
    s_nw = jnp.where(mask_n, s_nw - slope * dist_n, NEG)
    m = jnp.maximum(jnp.max(s_past, axis=-1, keepdims=True), jnp.max(s_nw, axis=-1, keepdims=True))
    p_past = jnp.where(mask_p, jnp.exp(s_past - m), 0.0)
    p_new = jnp.where(mask_n, jnp.exp(s_nw - m), 0.0)
    denom = jnp.sum(p_past, axis=-1, keepdims=True) + jnp.sum(p_new, axis=-1, keepdims=True)
    o_s = (_dot(p_past.astype(BF16), v_past) + _dot(p_new.astype(BF16), new[:, LANES:])) / jnp.maximum(denom, 1e-30)

    wc = wcache_ref[0, 0]
    win_buf = wc.shape[0]
    wnew = wnew_ref[0]
    wout_ref[0, 0, 0:win_buf - s_new, :] = wc[s_new:win_buf, :]
    wout_ref[0, 0, win_buf - s_new:win_buf, :] = wnew[0:s_new, :]
    wcb = wc.astype(BF16)
    wnb = _pad_rows(wnew, LANES).astype(BF16)
    kpos_w = (past_len - win_buf) + lax.broadcasted_iota(jnp.int32, (1, win_buf), 1)
    dist_wi = qpos - kpos_w
    mask_wp = (dist_wi >= 0) & (dist_wi < WINDOW)
    mask_wn = (dist_n >= 0.0) & (lane < S_PAD)
    s_wp = jnp.where(mask_wp, _dot_nt(qm, wcb[:, :LANES]) - slope * dist_wi.astype(F32), NEG)
    s_wn = jnp.where(mask_wn, _dot_nt(qm, wnb[:, :LANES]) - slope * dist_n, NEG)
    m = jnp.maximum(jnp.max(s_wp, axis=-1, keepdims=True), jnp.max(s_wn, axis=-1, keepdims=True))
    p_wp = jnp.where(mask_wp, jnp.exp(s_wp - m), 0.0)
    p_wn = jnp.where(mask_wn, jnp.exp(s_wn - m), 0.0)
    denom = jnp.sum(p_wp, axis=-1, keepdims=True) + jnp.sum(p_wn, axis=-1, keepdims=True)
    o_w = (_dot(p_wp.astype(BF16), wcb[:, LANES:]) + _dot(p_wn.astype(BF16), wnb[:, LANES:])) / jnp.maximum(denom, 1e-30)

    for g in range(NSA_KV_HEADS):
        pieces = []
        for r in range(NSA_GROUP):
            h = g * NSA_GROUP + r
            rs = slice(h * S_PAD, (h + 1) * S_PAD)
            pieces.append(gates[:, 3 * h:3 * h + 1] * o_c[rs] + gates[:, 3 * h + 1:3 * h + 2] * o_s[rs]
                          + gates[:, 3 * h + 2:3 * h + 3] * o_w[rs])
        onsa_ref[0, :, g * 256:(g + 1) * 256] = _pair_heads(pieces, g)

    q_sb = qs_ref[0]
    q_rep = jnp.concatenate([q_sb] * SB_HEADS, axis=0)
    lane_sb = lax.broadcasted_iota(jnp.int32, q_rep.shape, 1)
    row_sb = lax.broadcasted_iota(jnp.int32, q_rep.shape, 0)
    q_bd = jnp.where((lane_sb >> 6) == (row_sb >> 3), q_rep, jnp.zeros_like(q_rep))
    k_sb = jnp.concatenate([r[0, 0, :, 0:D_SB].astype(BF16) for r in sb_pages], axis=0)
    v_sb = jnp.concatenate([r[0, 0, :, D_SB:2 * D_SB].astype(BF16) for r in sb_pages], axis=0)
    sbn = _pad_rows(sbnew_ref[0], LANES).astype(BF16)
    z_past = _dot_nt(q_bd, k_sb)
    z_new = _dot_nt(q_bd, sbn[:, :D_SB])
    u = u_ref[...]
    mask_sn = lane < step
    lk = jnp.where(mask_sn, -_softplus(z_new), 0.0)
    later = _suffix_sums(lk, u)
    a_new = jnp.where(mask_sn, jnp.exp(z_new + lk + later), 0.0)
    carry = jnp.sum(lk, axis=-1, keepdims=True)
    a_tiles = [None] * n_pages
    for t in range(n_pages - 1, -1, -1):
        z = z_past[:, t * LANES:(t + 1) * LANES]
        lk = -_softplus(z)
        later = carry + _suffix_sums(lk, u)
        a_tiles[t] = jnp.exp(z + lk + later).astype(BF16)
        carry = carry + jnp.sum(lk, axis=-1, keepdims=True)
    o_all = _dot(jnp.concatenate(a_tiles, axis=1), v_sb) + _dot(a_new.astype(BF16), sbn[:, D_SB:])
    lane_o = lax.broadcasted_iota(jnp.int32, (S_PAD, D_SB), 1)
    o_sb = jnp.zeros((S_PAD, D_SB), F32)
    for h in range(SB_HEADS):
        o_sb = o_sb + jnp.where((lane_o >> 6) == h, o_all[h * S_PAD:(h + 1) * S_PAD], 0.0)
    osb_ref[0] = o_sb


def _sample_attend(page_table, lidx, qn, gn, kc, cache_slc, slc_new, cache_win, win_new, qs, cache_sb, sb_new,
                   win_buf_out, s_new):
    nb, n_pages = page_table.shape
    past_len = n_pages * PAGE_SIZE
    assert s_new < CMP_STRIDE and s_new <= S_PAD and past_len % SEL_BLOCK == 0
    n_sel = -(-(past_len + s_new) // SEL_BLOCK)
    n_cmp_pad = kc.shape[1]
    depth, _, win_buf, _ = cache_win.shape
    rows = NSA_HEADS * S_PAD
    m_mat = _sel_weights(n_cmp_pad, (past_len + s_new) // CMP_STRIDE - 1, n_sel)
    slope = jnp.asarray(np.repeat(np.asarray(SLOPES, np.float32), S_PAD).reshape(rows, 1))
    rr = np.arange(rows)
    gsum = jnp.asarray(((rr[:, None] // (S_PAD * NSA_GROUP) == rr[None, :] // (S_PAD * NSA_GROUP))
                        & (rr[:, None] % S_PAD == rr[None, :] % S_PAD)).astype(np.float32), BF16)
    expand = jnp.asarray((np.arange(LANES)[:, None] == np.arange(past_len)[None, :] // SEL_BLOCK).astype(np.float32),
                         BF16)
    aliased = win_buf_out is not None

    def const(shape):
        return pl.BlockSpec(shape, lambda b, pt, li: tuple(0 for _ in shape))

    def per_req(shape):
        return pl.BlockSpec((1,) + shape, lambda b, pt, li: (b,) + tuple(0 for _ in shape))

    def page_spec(i, width):
        return pl.BlockSpec((1, 1, PAGE_SIZE, width), lambda b, pt, li: (li[0], pt[b, i], 0, 0))

    in_specs = ([per_req((S_PAD, NSA_HEADS * LANES)), per_req((S_PAD, LANES)), per_req((n_cmp_pad, KV_W))]
                + [page_spec(i, KV_W) for i in range(n_pages)]
                + [per_req((S_PAD, KV_W)),
                   pl.BlockSpec((1, 1, win_buf, KV_W), lambda b, pt, li: (li[0], b, 0, 0)),
                   per_req((S_PAD, KV_W)), per_req((S_PAD, D_SB))]
                + [page_spec(i, 2 * D_SB) for i in range(n_pages)]
                + [per_req((S_PAD, 2 * D_SB)), const((n_cmp_pad, LANES)), const((rows, 1)), const((rows, rows)),
                   const((LANES, past_len)), const((LANES, LANES))])
    args = ([qn, gn, kc] + [cache_slc] * n_pages + [slc_new, cache_win, win_new, qs] + [cache_sb] * n_pages
            + [sb_new, m_mat, slope, gsum, expand, _later_matrix(LANES)])
    aliases = {}
    if aliased:
        in_specs.append(pl.BlockSpec(memory_space=pl.ANY))
        args.append(win_buf_out)
        aliases = {2 + len(args) - 1: 2}
    return pl.pallas_call(
        functools.partial(_sample_kernel, n_pages=n_pages, past_len=past_len, s_new=s_new, n_sel=n_sel,
                          aliased=aliased),
        grid_spec=pltpu.PrefetchScalarGridSpec(
            num_scalar_prefetch=2,
            grid=(nb,),
            in_specs=in_specs,
            out_specs=[per_req((S_PAD, D_NSA)), per_req((S_PAD, D_SB)),
                       pl.BlockSpec((1, 1, win_buf, KV_W), lambda b, pt, li: (li[0], b, 0, 0))]),
        out_shape=[jax.ShapeDtypeStruct((nb, S_PAD, D_NSA), F32), jax.ShapeDtypeStruct((nb, S_PAD, D_SB), F32),
                   jax.ShapeDtypeStruct((depth, nb, win_buf, KV_W), F32)],
        input_output_aliases=aliases,
        compiler_params=pltpu.CompilerParams(dimension_semantics=("arbitrary",), vmem_limit_bytes=VMEM_LIMIT),
        name="sample_attend",
    )(page_table, lidx, *args)


def _post_kernel(x_ref, onsa_ref, osb_ref, gm_ref, wout_ref, gpost_ref, gpre2_ref, w1_ref, w2_ref, gpost2_ref, y_ref):
    gm = gm_ref[...]
    y = (_dot((gm[:, :D_NSA] * onsa_ref[...]).astype(BF16), wout_ref[0:D_NSA, :])
         + _dot((gm[:, D_NSA:] * osb_ref[...]).astype(BF16), wout_ref[D_NSA:, :]))
    x = x_ref[...] + _rms(y, gpost_ref[...])
    hb = _rms(x, gpre2_ref[...]).astype(BF16)
    a = _dot(hb, w1_ref[:, 0:D_FF])
    b = _dot(hb, w1_ref[:, D_FF:])
    act = (a * _sigmoid(a)) * b
    y_ref[...] = x + _rms(_dot(act.astype(BF16), w2_ref[...]), gpost2_ref[...])


def _post(x2d, onsa, osb, gm, wout, gpost, gpre2, w1, w2, gpost2):
    n = x2d.shape[0]
    tm = min(ROW_TILE, n)

    def rows(w):
        return pl.BlockSpec((tm, w), lambda i: (i, 0))

    def whole(shape):
        return pl.BlockSpec(shape, lambda i: (0, 0), pipeline_mode=pl.Buffered(1))

    return pl.pallas_call(
        _post_kernel,
        grid=(n // tm,),
        in_specs=[rows(D_MODEL), rows(D_NSA), rows(D_SB), rows(D_MIX), whole((D_MIX, D_MODEL)), whole((1, D_MODEL)),
                  whole((1, D_MODEL)), whole((D_MODEL, 2 * D_FF)), whole((D_FF, D_MODEL)), whole((1, D_MODEL))],
        out_specs=rows(D_MODEL),
        out_shape=jax.ShapeDtypeStruct((n, D_MODEL), F32),
        compiler_params=pltpu.CompilerParams(dimension_semantics=("arbitrary",), vmem_limit_bytes=VMEM_LIMIT),
        name="post",
    )(x2d, onsa, osb, gm, wout, gpost, gpre2, w1, w2, gpost2)


def _relayout_w_in(w):
    z64 = jnp.zeros((D_MODEL, HEAD_DIM), w.dtype)
    cols = []
    for h in range(NSA_HEADS):
        wh = w[:, h * HEAD_DIM:(h + 1) * HEAD_DIM]
        cols += [wh, z64] if h // NSA_GROUP == 0 else [z64, wh]
    off = D_NSA
    cols.append(w[:, off:off + 3 * KV_W])
    off += 3 * KV_W
    cols.append(w[:, off:off + 3 * NSA_HEADS])
    cols.append(jnp.zeros((D_MODEL, LANES - 3 * NSA_HEADS), w.dtype))
    off += 3 * NSA_HEADS
    cols.append(w[:, off:])
    return jnp.concatenate(cols, axis=1).astype(BF16)


def _relayout_cmp(pe, w):
    eye_k = jnp.eye(2, dtype=w.dtype)
    eye_g = jnp.eye(NSA_KV_HEADS, dtype=w.dtype)
    outs = []
    for half in range(2):
        wh = w[:, half * CMP_STRIDE:(half + 1) * CMP_STRIDE]
        big = jnp.einsum('kpde,kK,gG->pkgdKGe', wh, eye_k, eye_g)
        outs.append(big.reshape(CMP_STRIDE * KV_W, KV_W).astype(BF16))
    pes = []
    for half in range(2):
        ph = pe[:, half * CMP_STRIDE:(half + 1) * CMP_STRIDE]
        ph = jnp.broadcast_to(ph.transpose(1, 0, 2)[:, :, None, :], (CMP_STRIDE, 2, NSA_KV_HEADS, HEAD_DIM))
        pes.append(ph.reshape(1, CMP_STRIDE * KV_W))
    return jnp.concatenate(pes, axis=0), outs[0], outs[1]


def kernel(x_prompt, x_sample, cache_cmp_kv, cache_slc_kv, cache_win_kv, cache_sb_kv, page_table, g_pre_mix,
           g_post_mix, g_pre_ffn, g_post_ffn, w_in, pe_cmp, w_cmp, w_out, w_ffn_in, w_ffn_out):
    depth = w_in.shape[0]
    bp, t, _ = x_prompt.shape
    bs, s_new, _ = x_sample.shape
    n_pool = cache_cmp_kv.shape[1]
    win_buf = cache_win_kv.shape[2]
    c_cmp = cache_cmp_kv.reshape(depth, n_pool, PAGE_SIZE, KV_W)
    c_slc = cache_slc_kv.reshape(depth, n_pool, PAGE_SIZE, KV_W)
    c_win = cache_win_kv.reshape(depth, bs, win_buf, KV_W)
    c_sb = cache_sb_kv.reshape(depth, n_pool, PAGE_SIZE, 2 * D_SB)

    xp = x_prompt.reshape(bp * t, D_MODEL)
    xs = jnp.pad(x_sample, ((0, 0), (0, S_PAD - s_new), (0, 0))).reshape(bs * S_PAD, D_MODEL)
    outs = {k: [] for k in ("cmp_p", "cmp_s", "slc_p", "slc_s", "win_p", "sb_p", "sb_s")}
    win_s = None
    for l in range(depth):
        w_cat = _relayout_w_in(w_in[l])
        pe2, wlo, whi = _relayout_cmp(pe_cmp[l], w_cmp[l])
        wout_b = w_out[l].astype(BF16)
        w1_b = w_ffn_in[l].astype(BF16)
        w2_b = w_ffn_out[l].astype(BF16)
        g1 = g_pre_mix[l].reshape(1, D_MODEL)
        g2 = g_post_mix[l].reshape(1, D_MODEL)
        g3 = g_pre_ffn[l].reshape(1, D_MODEL)
        g4 = g_post_ffn[l].reshape(1, D_MODEL)
        lidx = jnp.full((1,), l, jnp.int32)

        qn, kcmp, kslc, kwin, kslcb, kwinb, gn, qs, ksb, ksbb, gm = _inproj(xp, g1, w_cat)
        kc = _compress_prompt(kcmp.reshape(bp, t, KV_W), pe2, wlo, whi)
        o_nsa = _nsa_prompt(qn.reshape(bp, t, -1), gn.reshape(bp, t, -1), kc, kslcb.reshape(bp, t, KV_W),
                            kwinb.reshape(bp, t, KV_W))
        o_sb = _sb_prompt(qs.reshape(bp, t, D_SB), ksbb.reshape(bp, t, 2 * D_SB))
        xp = _post(xp, o_nsa.reshape(bp * t, D_NSA), o_sb.reshape(bp * t, D_SB), gm, wout_b, g2, g3, w1_b, w2_b, g4)
        outs["cmp_p"].append(kcmp.reshape(bp, t, 2, NSA_KV_HEADS, HEAD_DIM))
        outs["slc_p"].append(kslc.reshape(bp, t, 2, NSA_KV_HEADS, HEAD_DIM))
        outs["win_p"].append(kwin.reshape(bp, t, 2, NSA_KV_HEADS, HEAD_DIM)[:, t - min(WINDOW, t):])
        outs["sb_p"].append(ksb.reshape(bp, t, 2, SB_HEADS, HEAD_DIM))

        qn, kcmp, kslc, kwin, _, _, gn, qs, ksb, _, gm = _inproj(xs, g1, w_cat)
        kc = _compress_sample(c_cmp, page_table, lidx, pe2, wlo, whi)
        o_nsa, o_sb, win_s = _sample_attend(
            page_table, lidx, qn.reshape(bs, S_PAD, -1), gn.reshape(bs, S_PAD, -1), kc, c_slc,
            kslc.reshape(bs, S_PAD, KV_W), c_win, kwin.reshape(bs, S_PAD, KV_W), qs.reshape(bs, S_PAD, D_SB), c_sb,
            ksb.reshape(bs, S_PAD, 2 * D_SB), win_s, s_new)
        xs = _post(xs, o_nsa.reshape(bs * S_PAD, D_NSA), o_sb.reshape(bs * S_PAD, D_SB), gm, wout_b, g2, g3, w1_b,
                   w2_b, g4)
        outs["cmp_s"].append(kcmp.reshape(bs, S_PAD, 2, NSA_KV_HEADS, HEAD_DIM)[:, :s_new])
        outs["slc_s"].append(kslc.reshape(bs, S_PAD, 2, NSA_KV_HEADS, HEAD_DIM)[:, :s_new])
        outs["sb_s"].append(ksb.reshape(bs, S_PAD, 2, SB_HEADS, HEAD_DIM)[:, :s_new])

    y_p = xp.reshape(bp, t, D_MODEL)
    y_s = xs.reshape(bs, S_PAD, D_MODEL)[:, :s_new]
    win_s = win_s.reshape(depth, bs, win_buf, 2, NSA_KV_HEADS, HEAD_DIM)
    return (y_p, y_s, jnp.stack(outs["cmp_p"]), jnp.stack(outs["cmp_s"]), jnp.stack(outs["slc_p"]),
            jnp.stack(outs["slc_s"]), jnp.stack(outs["win_p"]), win_s, jnp.stack(outs["sb_p"]),
            jnp.stack(outs["sb_s"]))
```

```python
import functools

import numpy as np
import jax
import jax.numpy as jnp
from jax import lax
from jax.experimental import pallas as pl
from jax.experimental.pallas import tpu as pltpu

F32 = jnp.float32
BF16 = jnp.bfloat16

D_MODEL = 1024
HEAD_DIM = 64
NSA_HEADS = 8
NSA_KV_HEADS = 2
NSA_GROUP = NSA_HEADS // NSA_KV_HEADS
CMP_LEN = 32
CMP_STRIDE = 16
SEL_BLOCK = 64
SEL_TOP = 16
WINDOW = 512
SB_HEADS = 8
D_NSA = NSA_HEADS * HEAD_DIM
D_SB = SB_HEADS * HEAD_DIM
D_MIX = D_NSA + D_SB
D_FF = -(-8 * D_MODEL // (3 * 256)) * 256
KV_W = 2 * NSA_KV_HEADS * HEAD_DIM
PAGE_SIZE = 128
FORCE_BONUS = 1.0e4
NEG = -1.0e30
EPS = 1e-6
SLOPES = tuple(2.0 ** (-8.0 * (i + 1.0) / NSA_HEADS) for i in range(NSA_HEADS))

LANES = 128
S_PAD = 8
QB = 128
SEL_TILE = 512
SB_TQ = 256
SB_TK = 256
ROW_TILE = 256
VMEM_LIMIT = 56 * 1024 * 1024

C_QN = (0, 1024)
C_CMP = (1024, 1280)
C_SLC = (1280, 1536)
C_WIN = (1536, 1792)
C_GN = (1792, 1920)
C_QS = (1920, 2432)
C_SB = (2432, 3456)
C_GM = (3456, 4480)
W_CAT = 4480
R_CMP = (0, 256)
R_SLC = (256, 512)
R_WIN = (512, 768)
R_SB = (768, 1792)
W_KVT = 1792


def _dot(a, b):
    return jnp.dot(a, b, preferred_element_type=F32)


def _dot_nt(a, b):
    return lax.dot_general(a, b, (((1,), (1,)), ((), ())), preferred_element_type=F32)


def _split3(a):
    a1 = a.astype(BF16)
    r1 = a - a1.astype(F32)
    a2 = r1.astype(BF16)
    a3 = (r1 - a2.astype(F32)).astype(BF16)
    return a1, a2, a3


def _dot3_l(a, b):
    a1, a2, a3 = _split3(a)
    return _dot(a1, b) + _dot(a2, b) + _dot(a3, b)


def _dot3_r(a, b):
    b1, b2, b3 = _split3(b)
    return _dot(a, b1) + _dot(a, b2) + _dot(a, b3)


def _sigmoid(x):
    return 1.0 / (1.0 + jnp.exp(-x))


def _rms(x, g):
    return (x * lax.rsqrt(jnp.mean(x * x, axis=-1, keepdims=True) + EPS)) * g


def _masked_softmax(s, mask):
    s = jnp.where(mask, s, NEG)
    m = jnp.max(s, axis=-1, keepdims=True)
    p = jnp.where(mask, jnp.exp(s - m), 0.0)
    return p / jnp.maximum(jnp.sum(p, axis=-1, keepdims=True), 1e-30)


def _topk_select(score, n_blk):
    n_pad = -(-n_blk // 8) * 8
    st = score.T[:n_pad, :]
    j = lax.broadcasted_iota(jnp.int32, (n_pad, LANES), 0)
    rank = jnp.zeros((n_pad, LANES), F32)
    for i in range(n_blk):
        row = jnp.broadcast_to(st[i:i + 1, :], (n_pad, LANES))
        beats = (row > st) | ((row == st) & (j > i))
        rank = rank + jnp.where(beats, 1.0, 0.0)
    sel_t = jnp.where(rank < float(SEL_TOP), 1.0, 0.0)
    if n_pad < LANES:
        sel_t = jnp.concatenate([sel_t, jnp.zeros((LANES - n_pad, LANES), F32)], axis=0)
    return sel_t.T


def _pair_heads(pieces, g):
    lane = lax.broadcasted_iota(jnp.int32, pieces[0].shape, 1)
    outs = []
    for m in range(2):
        a, b = pieces[2 * m], pieces[2 * m + 1]
        if g == 0:
            left, right = a, pltpu.roll(b, 64, 1)
        else:
            left, right = pltpu.roll(a, 64, 1), b
        outs.append(jnp.where(lane < 64, left, right))
    return jnp.concatenate(outs, axis=1)


def _pad_rows(x, rows):
    return jnp.concatenate([x, jnp.zeros((rows - x.shape[0], x.shape[1]), x.dtype)], axis=0)


def _inproj_sample_kernel(x_ref, g_ref, w_ref, qn_ref, cmp_ref, slc_ref, win_ref, gn_ref, qs_ref, sb_ref, gm_ref):
    hb = _rms(x_ref[...], g_ref[...]).astype(BF16)

    def proj(sec):
        return _dot(hb, w_ref[:, sec[0]:sec[1]])

    qn_ref[...] = (proj(C_QN) * (HEAD_DIM ** -0.5)).astype(BF16)
    cmp_ref[...] = proj(C_CMP)
    slc_ref[...] = proj(C_SLC)
    win_ref[...] = proj(C_WIN)
    gn_ref[...] = _sigmoid(proj(C_GN))
    qs_ref[...] = (proj(C_QS) * (HEAD_DIM ** -0.5)).astype(BF16)
    sb_ref[...] = proj(C_SB)
    gm_ref[...] = _sigmoid(proj(C_GM))


def _inproj_sample(x2d, g, w_cat):
    n = x2d.shape[0]
    tm = min(ROW_TILE, n)
    widths = [(1024, BF16), (256, F32), (256, F32), (256, F32), (128, F32), (512, BF16), (1024, F32), (1024, F32)]
    return pl.pallas_call(
        _inproj_sample_kernel,
        grid=(n // tm,),
        in_specs=[pl.BlockSpec((tm, D_MODEL), lambda i: (i, 0)),
                  pl.BlockSpec((1, D_MODEL), lambda i: (0, 0)),
                  pl.BlockSpec((D_MODEL, W_CAT), lambda i: (0, 0))],
        out_specs=[pl.BlockSpec((tm, w), lambda i: (i, 0)) for w, _ in widths],
        out_shape=[jax.ShapeDtypeStruct((n, w), dt) for w, dt in widths],
        compiler_params=pltpu.CompilerParams(dimension_semantics=("arbitrary",), vmem_limit_bytes=VMEM_LIMIT),
        name="inproj_sample",
    )(x2d, g, w_cat)


def _inproj_prompt_kernel(x_ref, g_ref, w_ref, wt_ref, qn_ref, gn_ref, qs_ref, gm_ref, cmp_ref,
                          cmpt_ref, slct_ref, wint_ref, sbt_ref, slctb_ref, wintb_ref, sbtb_ref):
    hb = _rms(x_ref[0], g_ref[...]).astype(BF16)

    def proj(sec):
        return _dot(hb, w_ref[:, sec[0]:sec[1]])

    def proj_t(sec):
        return _dot_nt(wt_ref[sec[0]:sec[1], :], hb)

    qn_ref[0] = (proj(C_QN) * (HEAD_DIM ** -0.5)).astype(BF16)
    gn_ref[0] = _sigmoid(proj(C_GN))
    qs_ref[0] = (proj(C_QS) * (HEAD_DIM ** -0.5)).astype(BF16)
    gm_ref[0] = _sigmoid(proj(C_GM))
    cmp_ref[0] = proj(C_CMP)
    cmpt_ref[0] = proj_t(R_CMP)
    u = proj_t(R_SLC)
    slct_ref[0] = u
    slctb_ref[0] = u.astype(BF16)
    u = proj_t(R_WIN)
    wint_ref[0] = u
    wintb_ref[0] = u.astype(BF16)
    u = proj_t(R_SB)
    sbt_ref[0] = u
    sbtb_ref[0] = u.astype(BF16)


def _inproj_prompt(x3d, g, w_cat, w_kvt):
    b, t, _ = x3d.shape
    tm = ROW_TILE
    row_outs = [(1024, BF16), (128, F32), (512, BF16), (1024, F32), (256, F32)]
    col_outs = [(256, F32), (256, F32), (256, F32), (1024, F32), (256, BF16), (256, BF16), (1024, BF16)]
    return pl.pallas_call(
        _inproj_prompt_kernel,
        grid=(b, t // tm),
        in_specs=[pl.BlockSpec((1, tm, D_MODEL), lambda i, j: (i, j, 0)),
                  pl.BlockSpec((1, D_MODEL), lambda i, j: (0, 0)),
                  pl.BlockSpec((D_MODEL, W_CAT), lambda i, j: (0, 0)),
                  pl.BlockSpec((W_KVT, D_MODEL), lambda i, j: (0, 0))],
        out_specs=([pl.BlockSpec((1, tm, w), lambda i, j: (i, j, 0)) for w, _ in row_outs]
                   + [pl.BlockSpec((1, w, tm), lambda i, j: (i, 0, j)) for w, _ in col_outs]),
        out_shape=([jax.ShapeDtypeStruct((b, t, w), dt) for w, dt in row_outs]
                   + [jax.ShapeDtypeStruct((b, w, t), dt) for w, dt in col_outs]),
        compiler_params=pltpu.CompilerParams(dimension_semantics=("arbitrary", "arbitrary"),
                                             vmem_limit_bytes=VMEM_LIMIT),
        name="inproj_prompt",
    )(x3d, g, w_cat, w_kvt)


def _compress_kernel(*refs, n_pages, n_prefetch):
    refs = refs[n_prefetch:]
    page_refs = refs[:n_pages]
    pe_ref, wlo_ref, whi_ref, out_ref = refs[n_pages:]
    if n_pages == 1:
        x = page_refs[0][0]
    else:
        x = jnp.concatenate([r[0, 0] for r in page_refs], axis=0)
    rows = x.shape[0]
    lo = _dot((x + pe_ref[0:1, :]).astype(BF16), wlo_ref[...])
    hi = _dot((x + pe_ref[1:2, :]).astype(BF16), whi_ref[...])
    out = lo + pltpu.roll(hi, rows - 1, 0)
    c = lax.broadcasted_iota(jnp.int32, out.shape, 0)
    out_ref[0] = jnp.where(c < rows - 1, out, 0.0)


def _compress_prompt(kv_cmp, pe2, wlo, whi):
    b, t, _ = kv_cmp.shape
    rows = t // CMP_STRIDE
    x = kv_cmp.reshape(b, rows, CMP_STRIDE * KV_W)
    return pl.pallas_call(
        functools.partial(_compress_kernel, n_pages=1, n_prefetch=0),
        grid=(b,),
        in_specs=[pl.BlockSpec((1, rows, CMP_STRIDE * KV_W), lambda i: (i, 0, 0)),
                  pl.BlockSpec((2, CMP_STRIDE * KV_W), lambda i: (0, 0)),
                  pl.BlockSpec((CMP_STRIDE * KV_W, KV_W), lambda i: (0, 0)),
                  pl.BlockSpec((CMP_STRIDE * KV_W, KV_W), lambda i: (0, 0))],
        out_specs=pl.BlockSpec((1, rows, KV_W), lambda i: (i, 0, 0)),
        out_shape=jax.ShapeDtypeStruct((b, rows, KV_W), F32),
        compiler_params=pltpu.CompilerParams(dimension_semantics=("arbitrary",), vmem_limit_bytes=VMEM_LIMIT),
        name="compress_prompt",
    )(x, pe2, wlo, whi)


def _compress_sample(cache_cmp, page_table, lidx, pe2, wlo, whi):
    depth, n_pool = cache_cmp.shape[:2]
    nb, n_pages = page_table.shape
    cpp = PAGE_SIZE // CMP_STRIDE
    x = cache_cmp.reshape(depth, n_pool, cpp, CMP_STRIDE * KV_W)
    rows = n_pages * cpp

    def page_spec(i):
        return pl.BlockSpec((1, 1, cpp, CMP_STRIDE * KV_W), lambda b, pt, li: (li[0], pt[b, i], 0, 0))

    return pl.pallas_call(
        functools.partial(_compress_kernel, n_pages=n_pages, n_prefetch=2),
        grid_spec=pltpu.PrefetchScalarGridSpec(
            num_scalar_prefetch=2,
            grid=(nb,),
            in_specs=[page_spec(i) for i in range(n_pages)] + [
                pl.BlockSpec((2, CMP_STRIDE * KV_W), lambda b, pt, li: (0, 0)),
                pl.BlockSpec((CMP_STRIDE * KV_W, KV_W), lambda b, pt, li: (0, 0)),
                pl.BlockSpec((CMP_STRIDE * KV_W, KV_W), lambda b, pt, li: (0, 0))],
            out_specs=pl.BlockSpec((1, rows, KV_W), lambda b, pt, li: (b, 0, 0))),
        out_shape=jax.ShapeDtypeStruct((nb, rows, KV_W), F32),
        compiler_params=pltpu.CompilerParams(dimension_semantics=("arbitrary",), vmem_limit_bytes=VMEM_LIMIT),
        name="compress_sample",
    )(page_table, lidx, *([x] * n_pages), pe2, wlo, whi)


def _nsa_prompt_kernel(q_ref, gate_ref, kc_ref, slct_ref, wint_ref, m_ref, o_ref, m_scr, l_scr, acc_scr, *, n_sel):
    qb = pl.program_id(1)
    q0 = qb * QB
    q_all = q_ref[0]
    gates = gate_ref[0]
    kc = kc_ref[0]
    n_cmp_pad = kc.shape[0]
    kc2 = kc[:, :LANES].astype(BF16)
    vc2 = kc[:, LANES:].astype(BF16)
    qpos = q0 + lax.broadcasted_iota(jnp.int32, (QB, 1), 0)
    c_end = lax.broadcasted_iota(jnp.int32, (1, n_cmp_pad), 1) * CMP_STRIDE + (CMP_LEN - 1)
    dist_c = (qpos - c_end).astype(F32)
    mask_c = dist_c >= 0.0
    blk = lax.broadcasted_iota(jnp.int32, (QB, LANES), 1)
    cur = qpos >> 6
    allowed = (blk * SEL_BLOCK <= qpos) & (blk < n_sel)
    forced = (blk == 0) | (blk == cur) | (blk == cur - 1)
    w_start = pl.multiple_of(jnp.maximum(q0 - WINDOW, 0), QB)
    n_win = WINDOW + QB
    n_tiles = (q0 + QB + SEL_TILE - 1) // SEL_TILE

    for g in range(NSA_KV_HEADS):
        heads = [g * NSA_GROUP + r for r in range(NSA_GROUP)]
        qm = jnp.concatenate([q_all[:, h * LANES:(h + 1) * LANES] for h in heads], axis=0)

        s = _dot_nt(qm, kc2)
        p_sum = jnp.zeros((QB, n_cmp_pad), F32)
        p_rows = []
        for r, h in enumerate(heads):
            p = _masked_softmax(s[r * QB:(r + 1) * QB] - SLOPES[h] * dist_c, mask_c)
            p_sum = p_sum + p
            p_rows.append(p.astype(BF16))
        o_c = _dot(jnp.concatenate(p_rows, axis=0), vc2)

        imp = _dot3_l(p_sum, m_ref[...])
        score = jnp.where(allowed, imp + jnp.where(forced, FORCE_BONUS, 0.0), -FORCE_BONUS)
        score = jnp.where(blk < n_sel, score, -3.0 * FORCE_BONUS)
        sel = jnp.where(allowed, _topk_select(score, n_sel), 0.0).astype(BF16)

        m_scr[...] = jnp.full(m_scr.shape, NEG, F32)
        l_scr[...] = jnp.zeros(l_scr.shape, F32)
        acc_scr[...] = jnp.zeros(acc_scr.shape, F32)

        def sel_tile(kt, carry, qm=qm, sel=sel, heads=heads):
            k0 = pl.multiple_of(kt * SEL_TILE, SEL_TILE)
            k2t = slct_ref[0, 0:LANES, pl.ds(k0, SEL_TILE)]
            v2t = slct_ref[0, LANES:2 * LANES, pl.ds(k0, SEL_TILE)]
            s = _dot(qm, k2t)
            kpos = k0 + lax.broadcasted_iota(jnp.int32, (1, SEL_TILE), 1)
            dist = (qpos - kpos).astype(F32)
            e_row = lax.broadcasted_iota(jnp.int32, (LANES, SEL_TILE), 0)
            e_col = lax.broadcasted_iota(jnp.int32, (LANES, SEL_TILE), 1)
            expand = jnp.where(e_row == kt * (SEL_TILE // SEL_BLOCK) + (e_col >> 6), 1.0, 0.0).astype(BF16)
            mask = (_dot(sel, expand) > 0.5) & (dist >= 0.0)
            p_rows = []
            for r, h in enumerate(heads):
                rows = slice(r * QB, (r + 1) * QB)
                sr = jnp.where(mask, s[rows] - SLOPES[h] * dist, 2.0 * NEG)
                m_old = m_scr[rows]
                m_new = jnp.maximum(m_old, jnp.max(sr, axis=-1, keepdims=True))
                alpha = jnp.exp(m_old - m_new)
                p = jnp.exp(sr - m_new)
                l_scr[rows] = alpha * l_scr[rows] + jnp.sum(p, axis=-1, keepdims=True)
                m_scr[rows] = m_new
                acc_scr[rows] = alpha * acc_scr[rows]
                p_rows.append(p.astype(BF16))
            acc_scr[...] += _dot_nt(jnp.concatenate(p_rows, axis=0), v2t)
            return carry

        lax.fori_loop(0, n_tiles, sel_tile, 0)
        o_s = acc_scr[...] / jnp.maximum(l_scr[...], 1e-30)

        kwt = wint_ref[0, :, pl.ds(w_start, n_win)]
        s = _dot(qm, kwt[:LANES])
        kpos = w_start + lax.broadcasted_iota(jnp.int32, (1, n_win), 1)
        dist_i = qpos - kpos
        mask_w = (dist_i >= 0) & (dist_i < WINDOW)
        dist_w = dist_i.astype(F32)
        p_rows = []
        for r, h in enumerate(heads):
            p_rows.append(_masked_softmax(s[r * QB:(r + 1) * QB] - SLOPES[h] * dist_w, mask_w).astype(BF16))
        o_w = _dot_nt(jnp.concatenate(p_rows, axis=0), kwt[LANES:])

        pieces = []
        for r, h in enumerate(heads):
            rows = slice(r * QB, (r + 1) * QB)
            pieces.append(gates[:, 3 * h:3 * h + 1] * o_c[rows] + gates[:, 3 * h + 1:3 * h + 2] * o_s[rows]
                          + gates[:, 3 * h + 2:3 * h + 3] * o_w[rows])
        o_ref[0, :, g * 256:(g + 1) * 256] = _pair_heads(pieces, g)


def _sel_weights(n_cmp_pad, n_cmp, n_sel):
    ratio = SEL_BLOCK // CMP_STRIDE
    m = np.zeros((n_cmp_pad, LANES), np.float32)
    for j in range(n_sel):
        for off in range(-1, ratio):
            c = ratio * j + off
            if 0 <= c < n_cmp:
                m[c, j] += 1.0 if off in (-1, ratio - 1) else 2.0
    return jnp.asarray(m, BF16)


def _nsa_prompt(qn, gn, kc, slctb, wintb):
    b, t, _ = qn.shape
    n_sel = -(-t // SEL_BLOCK)
    n_cmp_pad = kc.shape[1]
    m_mat = _sel_weights(n_cmp_pad, t // CMP_STRIDE - 1, n_sel)
    return pl.pallas_call(
        functools.partial(_nsa_prompt_kernel, n_sel=n_sel),
        grid=(b, t // QB),
        in_specs=[pl.BlockSpec((1, QB, NSA_HEADS * LANES), lambda i, j: (i, j, 0)),
                  pl.BlockSpec((1, QB, LANES), lambda i, j: (i, j, 0)),
                  pl.BlockSpec((1, n_cmp_pad, KV_W), lambda i, j: (i, 0, 0)),
                  pl.BlockSpec((1, KV_W, t), lambda i, j: (i, 0, 0)),
                  pl.BlockSpec((1, KV_W, t), lambda i, j: (i, 0, 0)),
                  pl.BlockSpec((n_cmp_pad, LANES), lambda i, j: (0, 0))],
        out_specs=pl.BlockSpec((1, QB, D_NSA), lambda i, j: (i, j, 0)),
        out_shape=jax.ShapeDtypeStruct((b, t, D_NSA), F32),
        scratch_shapes=[pltpu.VMEM((NSA_GROUP * QB, 1), F32), pltpu.VMEM((NSA_GROUP * QB, 1), F32),
                        pltpu.VMEM((NSA_GROUP * QB, LANES), F32)],
        compiler_params=pltpu.CompilerParams(dimension_semantics=("arbitrary", "arbitrary"),
                                             vmem_limit_bytes=VMEM_LIMIT),
        name="nsa_prompt",
    )(qn, gn, kc, slctb, wintb, m_mat)


def _softplus(z):
    neg_abs = lax.bitcast_convert_type(lax.bitcast_convert_type(z, jnp.int32) | jnp.int32(-2 ** 31), F32)
    return jnp.maximum(z, 0.0) + jnp.log(1.0 + jnp.exp(neg_abs))


def _suffix_sums(x, u2):
    hi = x.astype(BF16)
    lo = (x - hi.astype(F32)).astype(BF16)
    return _dot(jnp.concatenate([hi, lo], axis=1), u2)


def _sb_prompt_kernel(q_ref, kt_ref, vt_ref, u_ref, o_ref, c_scr, acc_scr, z_scr, a_scr):
    qb = pl.program_id(2)
    q2 = q_ref[0]
    lane = lax.broadcasted_iota(jnp.int32, q2.shape, 1)
    qm = jnp.concatenate([jnp.where((lane >> 6) == h, q2, jnp.zeros_like(q2)) for h in range(2)], axis=0)
    c_scr[...] = jnp.zeros(c_scr.shape, F32)
    acc_scr[...] = jnp.zeros(acc_scr.shape, F32)
    u2 = u_ref[...]

    def scores(kt):
        k0 = pl.multiple_of(kt * SB_TK, SB_TK)
        return _dot(qm, kt_ref[0, :, pl.ds(k0, SB_TK)])

    def add_values(kt):
        k0 = pl.multiple_of(kt * SB_TK, SB_TK)
        acc_scr[...] += _dot_nt(a_scr[...], vt_ref[0, :, pl.ds(k0, SB_TK)])

    def weights(z, mask):
        sp = _softplus(z)
        if mask is not None:
            sp = jnp.where(mask, sp, 0.0)
        a = jnp.exp((z - sp) - (c_scr[...] + _suffix_sums(sp, u2)))
        if mask is not None:
            a = jnp.where(mask, a, 0.0)
        c_scr[...] += jnp.sum(sp, axis=-1, keepdims=True)
        return a.astype(BF16)

    row = lax.broadcasted_iota(jnp.int32, (2 * SB_TQ, SB_TK), 0) & (SB_TQ - 1)
    col = lax.broadcasted_iota(jnp.int32, (2 * SB_TQ, SB_TK), 1)
    z_scr[...] = scores(jnp.maximum(qb - 1, 0))
    a_scr[...] = weights(scores(qb), col < row)

    def earlier(i, carry):
        kt = qb - 1 - i
        z = z_scr[...]
        add_values(kt + 1)
        z_scr[...] = scores(jnp.maximum(kt - 1, 0))
        a_scr[...] = weights(z, None)
        return carry

    lax.fori_loop(0, qb, earlier, 0)
    add_values(0)
    lane_o = lax.broadcasted_iota(jnp.int32, (SB_TQ, LANES), 1)
    o_ref[0] = jnp.where(lane_o < HEAD_DIM, acc_scr[0:SB_TQ], acc_scr[SB_TQ:])


def _later_matrix(n):
    u = np.tril(np.ones((n, n), np.float32), -1)
    return jnp.asarray(np.concatenate([u, u], axis=0), BF16)


def _sb_prompt(qs, sbtb):
    b, t, _ = qs.shape
    n_hp = SB_HEADS // 2
    return pl.pallas_call(
        _sb_prompt_kernel,
        grid=(b, n_hp, t // SB_TQ),
        in_specs=[pl.BlockSpec((1, SB_TQ, LANES), lambda i, h, j: (i, j, h)),
                  pl.BlockSpec((1, LANES, t), lambda i, h, j: (i, h, 0)),
                  pl.BlockSpec((1, LANES, t), lambda i, h, j: (i, n_hp + h, 0)),
                  pl.BlockSpec((2 * SB_TK, SB_TK), lambda i, h, j: (0, 0))],
        out_specs=pl.BlockSpec((1, SB_TQ, LANES), lambda i, h, j: (i, j, h)),
        out_shape=jax.ShapeDtypeStruct((b, t, D_SB), F32),
        scratch_shapes=[pltpu.VMEM((2 * SB_TQ, 1), F32), pltpu.VMEM((2 * SB_TQ, LANES), F32),
                        pltpu.VMEM((2 * SB_TQ, SB_TK), F32), pltpu.VMEM((2 * SB_TQ, SB_TK), BF16)],
        compiler_params=pltpu.CompilerParams(dimension_semantics=("arbitrary", "arbitrary", "arbitrary"),
                                             vmem_limit_bytes=VMEM_LIMIT),
        name="sb_prompt",
    )(qs, sbtb, sbtb, _later_matrix(SB_TK))


def _nsa_sample_kernel(*refs, n_pages, past_len, s_new, n_sel, aliased):
    refs = refs[2:]
    qn_ref, gate_ref, kc_ref = refs[:3]
    slc_pages = refs[3:3 + n_pages]
    refs = refs[3 + n_pages:]
    slcnew_ref, wcache_ref, wnew_ref, m_ref, slope_ref, gsum_ref, expand_ref = refs[:7]
    refs = refs[7:]
    if aliased:
        refs = refs[1:]
    onsa_ref, wout_ref = refs

    rows = NSA_HEADS * S_PAD
    row = lax.broadcasted_iota(jnp.int32, (rows, 1), 0)
    step = row & (S_PAD - 1)
    qpos = past_len + step
    slope = slope_ref[...]
    gates = gate_ref[0]
    q_all = qn_ref[0]
    qm = jnp.concatenate([q_all[:, h * LANES:(h + 1) * LANES] for h in range(NSA_HEADS)], axis=0)
    lane = lax.broadcasted_iota(jnp.int32, (rows, LANES), 1)

    kc = kc_ref[0]
    n_cmp_pad = kc.shape[0]
    c_end = lax.broadcasted_iota(jnp.int32, (1, n_cmp_pad), 1) * CMP_STRIDE + (CMP_LEN - 1)
    dist_c = (qpos - c_end).astype(F32)
    p_c = _masked_softmax(_dot_nt(qm, kc[:, :LANES].astype(BF16)) - slope * dist_c, dist_c >= 0.0)
    o_c = _dot(p_c.astype(BF16), kc[:, LANES:].astype(BF16))

    p_sum = _dot3_r(gsum_ref[...], p_c)
    imp = _dot3_l(p_sum, m_ref[...])
    cur = qpos >> 6
    allowed = (lane * SEL_BLOCK <= qpos) & (lane < n_sel)
    forced = (lane == 0) | (lane == cur) | (lane == cur - 1)
    score = jnp.where(allowed, imp + jnp.where(forced, FORCE_BONUS, 0.0), -FORCE_BONUS)
    score = jnp.where(lane < n_sel, score, -3.0 * FORCE_BONUS)
    sel = _topk_select(_pad_rows(score, LANES), n_sel)[:rows]
    sel = jnp.where(allowed, sel, 0.0)

    k2t = jnp.concatenate([r[0, 0, 0].reshape(LANES, PAGE_SIZE).astype(BF16) for r in slc_pages], axis=1)
    v2t = jnp.concatenate([r[0, 0, 1].reshape(LANES, PAGE_SIZE).astype(BF16) for r in slc_pages], axis=1)
    new = _pad_rows(slcnew_ref[0], LANES).astype(BF16)
    s_past = _dot(qm, k2t)
    s_nw = _dot_nt(qm, new[:, :LANES])
    kpos = lax.broadcasted_iota(jnp.int32, (1, past_len), 1)
    dist_p = (qpos - kpos).astype(F32)
    mask_p = (_dot(sel.astype(BF16), expand_ref[...]) > 0.5) & (dist_p >= 0.0)
    dist_n = (step - lane).astype(F32)
    last_blk = past_len // SEL_BLOCK
    sel_last = jnp.sum(jnp.where(lane == last_blk, sel, 0.0), axis=-1, keepdims=True)
    mask_n = (dist_n >= 0.0) & (sel_last > 0.5) & (lane < S_PAD)
    s_past = jnp.where(mask_p, s_past - slope * dist_p, NEG)
    s_nw = jnp.where(mask_n, s_nw - slope * dist_n, NEG)
    m = jnp.maximum(jnp.max(s_past, axis=-1, keepdims=True), jnp.max(s_nw, axis=-1, keepdims=True))
    p_past = jnp.where(mask_p, jnp.exp(s_past - m), 0.0)
    p_new = jnp.where(mask_n, jnp.exp(s_nw - m), 0.0)
    denom = jnp.sum(p_past, axis=-1, keepdims=True) + jnp.sum(p_new, axis=-1, keepdims=True)
    o_s = (_dot_nt(p_past.astype(BF16), v2t) + _dot(p_new.astype(BF16), new[:, LANES:])) / jnp.maximum(denom, 1e-30)

    wc = wcache_ref[0, 0]
    win_buf = wc.shape[-1]
    wct = wc.reshape(KV_W, win_buf)
    wnew = wnew_ref[0]
    wnew_pad = _pad_rows(wnew, LANES)
    shifted = pltpu.roll(wct, win_buf - s_new, 1)
    tail = pltpu.roll(wnew_pad.T, LANES - s_new, 1)
    lane_w = lax.broadcasted_iota(jnp.int32, (KV_W, LANES), 1)
    last = jnp.where(lane_w >= LANES - s_new, tail, shifted[:, win_buf - LANES:])
    wout_ref[0, 0] = jnp.concatenate([shifted[:, :win_buf - LANES], last], axis=1).reshape(wc.shape)
    wcb = wct.astype(BF16)
    wnb = wnew_pad.astype(BF16)
    kpos_w = (past_len - win_buf) + lax.broadcasted_iota(jnp.int32, (1, win_buf), 1)
    dist_wi = qpos - kpos_w
    mask_wp = (dist_wi >= 0) & (dist_wi < WINDOW)
    mask_wn = (dist_n >= 0.0) & (lane < S_PAD)
    s_wp = jnp.where(mask_wp, _dot(qm, wcb[:LANES]) - slope * dist_wi.astype(F32), NEG)
    s_wn = jnp.where(mask_wn, _dot_nt(qm, wnb[:, :LANES]) - slope * dist_n, NEG)
    m = jnp.maximum(jnp.max(s_wp, axis=-1, keepdims=True), jnp.max(s_wn, axis=-1, keepdims=True))
    p_wp = jnp.where(mask_wp, jnp.exp(s_wp - m), 0.0)
    p_wn = jnp.where(mask_wn, jnp.exp(s_wn - m), 0.0)
    denom = jnp.sum(p_wp, axis=-1, keepdims=True) + jnp.sum(p_wn, axis=-1, keepdims=True)
    o_w = (_dot_nt(p_wp.astype(BF16), wcb[LANES:]) + _dot(p_wn.astype(BF16), wnb[:, LANES:])) / jnp.maximum(denom, 1e-30)

    for g in range(NSA_KV_HEADS):
        pieces = []
        for r in range(NSA_GROUP):
            h = g * NSA_GROUP + r
            rs = slice(h * S_PAD, (h + 1) * S_PAD)
            pieces.append(gates[:, 3 * h:3 * h + 1] * o_c[rs] + gates[:, 3 * h + 1:3 * h + 2] * o_s[rs]
                          + gates[:, 3 * h + 2:3 * h + 3] * o_w[rs])
        onsa_ref[0, :, g * 256:(g + 1) * 256] = _pair_heads(pieces, g)


def _nsa_sample(page_table, lidx, qn, gn, kc, cache_slct, slc_new, cache_wint, win_new, win_buf_out, s_new):
    nb, n_pages = page_table.shape
    past_len = n_pages * PAGE_SIZE
    assert s_new < CMP_STRIDE and s_new <= S_PAD and past_len % SEL_BLOCK == 0
    n_sel = -(-(past_len + s_new) // SEL_BLOCK)
    n_cmp_pad = kc.shape[1]
    depth, _, _, _, _, win_buf = cache_wint.shape
    rows = NSA_HEADS * S_PAD
    m_mat = _sel_weights(n_cmp_pad, (past_len + s_new) // CMP_STRIDE - 1, n_sel)
    slope = jnp.asarray(np.repeat(np.asarray(SLOPES, np.float32), S_PAD).reshape(rows, 1))
    rr = np.arange(rows)
    gsum = jnp.asarray(((rr[:, None] // (S_PAD * NSA_GROUP) == rr[None, :] // (S_PAD * NSA_GROUP))
                        & (rr[:, None] % S_PAD == rr[None, :] % S_PAD)).astype(np.float32), BF16)
    expand = jnp.asarray((np.arange(LANES)[:, None] == np.arange(past_len)[None, :] // SEL_BLOCK).astype(np.float32),
                         BF16)
    aliased = win_buf_out is not None
    kv_dims = (2, NSA_KV_HEADS, HEAD_DIM)

    def const(shape):
        return pl.BlockSpec(shape, lambda b, pt, li: tuple(0 for _ in shape))

    def per_req(shape):
        return pl.BlockSpec((1,) + shape, lambda b, pt, li: (b,) + tuple(0 for _ in shape))

    def page_spec(i):
        return pl.BlockSpec((1, 1) + kv_dims + (PAGE_SIZE,), lambda b, pt, li: (li[0], pt[b, i], 0, 0, 0, 0))

    win_spec = pl.BlockSpec((1, 1) + kv_dims + (win_buf,), lambda b, pt, li: (li[0], b, 0, 0, 0, 0))
    in_specs = ([per_req((S_PAD, NSA_HEADS * LANES)), per_req((S_PAD, LANES)), per_req((n_cmp_pad, KV_W))]
                + [page_spec(i) for i in range(n_pages)]
                + [per_req((S_PAD, KV_W)), win_spec, per_req((S_PAD, KV_W)),
                   const((n_cmp_pad, LANES)), const((rows, 1)), const((rows, rows)), const((LANES, past_len))])
    args = [qn, gn, kc] + [cache_slct] * n_pages + [slc_new, cache_wint, win_new, m_mat, slope, gsum, expand]
    aliases = {}
    if aliased:
        in_specs.append(pl.BlockSpec(memory_space=pl.ANY))
        args.append(win_buf_out)
        aliases = {2 + len(args) - 1: 1}
    return pl.pallas_call(
        functools.partial(_nsa_sample_kernel, n_pages=n_pages, past_len=past_len, s_new=s_new, n_sel=n_sel,
                          aliased=aliased),
        grid_spec=pltpu.PrefetchScalarGridSpec(
            num_scalar_prefetch=2,
            grid=(nb,),
            in_specs=in_specs,
            out_specs=[per_req((S_PAD, D_NSA)), win_spec]),
        out_shape=[jax.ShapeDtypeStruct((nb, S_PAD, D_NSA), F32),
                   jax.ShapeDtypeStruct((depth, nb) + kv_dims + (win_buf,), F32)],
        input_output_aliases=aliases,
        compiler_params=pltpu.CompilerParams(dimension_semantics=("arbitrary",), vmem_limit_bytes=VMEM_LIMIT),
        name="nsa_sample",
    )(page_table, lidx, *args)


def _sb_sample_kernel(*refs, n_pages):
    refs = refs[2:]
    qs_ref = refs[0]
    sb_pages = refs[1:1 + n_pages]
    sbnew_ref, u_ref, osb_ref = refs[1 + n_pages:]

    rows = SB_HEADS * S_PAD
    step = lax.broadcasted_iota(jnp.int32, (rows, 1), 0) & (S_PAD - 1)
    lane = lax.broadcasted_iota(jnp.int32, (rows, LANES), 1)
    q_sb = qs_ref[0]
    q_rep = jnp.concatenate([q_sb] * SB_HEADS, axis=0)
    lane_sb = lax.broadcasted_iota(jnp.int32, q_rep.shape, 1)
    row_sb = lax.broadcasted_iota(jnp.int32, q_rep.shape, 0)
    q_bd = jnp.where((lane_sb >> 6) == (row_sb >> 3), q_rep, jnp.zeros_like(q_rep))
    kt = jnp.concatenate([r[0, 0, 0].reshape(D_SB, PAGE_SIZE).astype(BF16) for r in sb_pages], axis=1)
    vt = jnp.concatenate([r[0, 0, 1].reshape(D_SB, PAGE_SIZE).astype(BF16) for r in sb_pages], axis=1)
    sbn = _pad_rows(sbnew_ref[0], LANES).astype(BF16)
    z_past = _dot(q_bd, kt)
    z_new = _dot_nt(q_bd, sbn[:, :D_SB])
    u = u_ref[...]
    mask_sn = lane < step
    sp = jnp.where(mask_sn, _softplus(z_new), 0.0)
    a_new = jnp.where(mask_sn, jnp.exp((z_new - sp) - _suffix_sums(sp, u)), 0.0)
    carry = jnp.sum(sp, axis=-1, keepdims=True)
    a_tiles = [None] * n_pages
    for t in range(n_pages - 1, -1, -1):
        z = z_past[:, t * LANES:(t + 1) * LANES]
        sp = _softplus(z)
        a_tiles[t] = jnp.exp((z - sp) - (carry + _suffix_sums(sp, u))).astype(BF16)
        carry = carry + jnp.sum(sp, axis=-1, keepdims=True)
    o_all = _dot_nt(jnp.concatenate(a_tiles, axis=1), vt) + _dot(a_new.astype(BF16), sbn[:, D_SB:])
    lane_o = lax.broadcasted_iota(jnp.int32, (S_PAD, D_SB), 1)
    o_sb = jnp.zeros((S_PAD, D_SB), F32)
    for h in range(SB_HEADS):
        o_sb = o_sb + jnp.where((lane_o >> 6) == h, o_all[h * S_PAD:(h + 1) * S_PAD], 0.0)
    osb_ref[0] = o_sb


def _sb_sample(page_table, lidx, qs, cache_sbt, sb_new):
    nb, n_pages = page_table.shape
    kv_dims = (2, SB_HEADS, HEAD_DIM)

    def per_req(shape):
        return pl.BlockSpec((1,) + shape, lambda b, pt, li: (b,) + tuple(0 for _ in shape))

    def page_spec(i):
        return pl.BlockSpec((1, 1) + kv_dims + (PAGE_SIZE,), lambda b, pt, li: (li[0], pt[b, i], 0, 0, 0, 0))

    in_specs = ([per_req((S_PAD, D_SB))] + [page_spec(i) for i in range(n_pages)]
                + [per_req((S_PAD, 2 * D_SB)), pl.BlockSpec((2 * LANES, LANES), lambda b, pt, li: (0, 0))])
    return pl.pallas_call(
        functools.partial(_sb_sample_kernel, n_pages=n_pages),
        grid_spec=pltpu.PrefetchScalarGridSpec(
            num_scalar_prefetch=2,
            grid=(nb,),
            in_specs=in_specs,
            out_specs=per_req((S_PAD, D_SB))),
        out_shape=jax.ShapeDtypeStruct((nb, S_PAD, D_SB), F32),
        compiler_params=pltpu.CompilerParams(dimension_semantics=("arbitrary",), vmem_limit_bytes=VMEM_LIMIT),
        name="sb_sample",
    )(page_table, lidx, qs, *([cache_sbt] * n_pages), sb_new, _later_matrix(LANES))


def _post_kernel(x_ref, onsa_ref, osb_ref, gm_ref, wout_ref, gpost_ref, gpre2_ref, w1_ref, w2_ref, gpost2_ref, y_ref):
    gm = gm_ref[...]
    y = (_dot((gm[:, :D_NSA] * onsa_ref[...]).astype(BF16), wout_ref[0:D_NSA, :])
         + _dot((gm[:, D_NSA:] * osb_ref[...]).astype(BF16), wout_ref[D_NSA:, :]))
    x = x_ref[...] + _rms(y, gpost_ref[...])
    hb = _rms(x, gpre2_ref[...]).astype(BF16)
    a = _dot(hb, w1_ref[:, 0:D_FF])
    b = _dot(hb, w1_ref[:, D_FF:])
    act = (a * _sigmoid(a)) * b
    y_ref[...] = x + _rms(_dot(act.astype(BF16), w2_ref[...]), gpost2_ref[...])


def _post(x2d, onsa, osb, gm, wout, gpost, gpre2, w1, w2, gpost2):
    n = x2d.shape[0]
    tm = min(ROW_TILE, n)

    def rows(w):
        return pl.BlockSpec((tm, w), lambda i: (i, 0))

    def whole(shape):
        return pl.BlockSpec(shape, lambda i: (0, 0), pipeline_mode=pl.Buffered(1))

    return pl.pallas_call(
        _post_kernel,
        grid=(n // tm,),
        in_specs=[rows(D_MODEL), rows(D_NSA), rows(D_SB), rows(D_MIX), whole((D_MIX, D_MODEL)), whole((1, D_MODEL)),
                  whole((1, D_MODEL)), whole((D_MODEL, 2 * D_FF)), whole((D_FF, D_MODEL)), whole((1, D_MODEL))],
        out_specs=rows(D_MODEL),
        out_shape=jax.ShapeDtypeStruct((n, D_MODEL), F32),
        compiler_params=pltpu.CompilerParams(dimension_semantics=("arbitrary",), vmem_limit_bytes=VMEM_LIMIT),
        name="post",
    )(x2d, onsa, osb, gm, wout, gpost, gpre2, w1, w2, gpost2)


def _relayout_w_in(w):
    z64 = jnp.zeros((D_MODEL, HEAD_DIM), w.dtype)
    cols = []
    for h in range(NSA_HEADS):
        wh = w[:, h * HEAD_DIM:(h + 1) * HEAD_DIM]
        cols += [wh, z64] if h // NSA_GROUP == 0 else [z64, wh]
    off = D_NSA
    cols.append(w[:, off:off + 3 * KV_W])
    off += 3 * KV_W
    cols.append(w[:, off:off + 3 * NSA_HEADS])
    cols.append(jnp.zeros((D_MODEL, LANES - 3 * NSA_HEADS), w.dtype))
    off += 3 * NSA_HEADS
    cols.append(w[:, off:])
    w_cat = jnp.concatenate(cols, axis=1).astype(BF16)
    w_kvt = jnp.concatenate([w_cat[:, C_CMP[0]:C_WIN[1]], w_cat[:, C_SB[0]:C_SB[1]]], axis=1).T
    return w_cat, w_kvt


def _relayout_cmp(pe, w):
    eye_k = jnp.eye(2, dtype=w.dtype)
    eye_g = jnp.eye(NSA_KV_HEADS, dtype=w.dtype)
    outs = []
    for half in range(2):
        wh = w[:, half * CMP_STRIDE:(half + 1) * CMP_STRIDE]
        big = jnp.einsum('kpde,kK,gG->pkgdKGe', wh, eye_k, eye_g)
        outs.append(big.reshape(CMP_STRIDE * KV_W, KV_W).astype(BF16))
    pes = []
    for half in range(2):
        ph = pe[:, half * CMP_STRIDE:(half + 1) * CMP_STRIDE]
        ph = jnp.broadcast_to(ph.transpose(1, 0, 2)[:, :, None, :], (CMP_STRIDE, 2, NSA_KV_HEADS, HEAD_DIM))
        pes.append(ph.reshape(1, CMP_STRIDE * KV_W))
    return jnp.concatenate(pes, axis=0), outs[0], outs[1]


def _rows_major(xt, lead, n_heads):
    t = xt.shape[-1]
    k = len(lead)
    xt = xt.reshape(lead + (2, n_heads, HEAD_DIM, t))
    return xt.transpose(tuple(range(k)) + (k + 3, k, k + 1, k + 2))


def kernel(x_prompt, x_sample, cache_cmp_kv, cache_slc_kv, cache_win_kv, cache_sb_kv, page_table, g_pre_mix,
           g_post_mix, g_pre_ffn, g_post_ffn, w_in, pe_cmp, w_cmp, w_out, w_ffn_in, w_ffn_out):
    depth = w_in.shape[0]
    bp, t, _ = x_prompt.shape
    bs, s_new, _ = x_sample.shape
    n_pool = cache_cmp_kv.shape[1]
    c_cmp = cache_cmp_kv.reshape(depth, n_pool, PAGE_SIZE, KV_W)
    c_slct = cache_slc_kv.transpose(0, 1, 3, 4, 5, 2)
    c_wint = cache_win_kv.transpose(0, 1, 3, 4, 5, 2)
    c_sbt = cache_sb_kv.transpose(0, 1, 3, 4, 5, 2)

    xp = x_prompt
    xs = jnp.pad(x_sample, ((0, 0), (0, S_PAD - s_new), (0, 0))).reshape(bs * S_PAD, D_MODEL)
    outs = {k: [] for k in ("cmp_p", "cmp_s", "slc_p", "slc_s", "win_p", "sb_p", "sb_s")}
    win_s = None
    for l in range(depth):
        w_cat, w_kvt = _relayout_w_in(w_in[l])
        pe2, wlo, whi = _relayout_cmp(pe_cmp[l], w_cmp[l])
        wout_b = w_out[l].astype(BF16)
        w1_b = w_ffn_in[l].astype(BF16)
        w2_b = w_ffn_out[l].astype(BF16)
        g1 = g_pre_mix[l].reshape(1, D_MODEL)
        g2 = g_post_mix[l].reshape(1, D_MODEL)
        g3 = g_pre_ffn[l].reshape(1, D_MODEL)
        g4 = g_post_ffn[l].reshape(1, D_MODEL)
        lidx = jnp.full((1,), l, jnp.int32)

        qn, gn, qs, gm, kcmp, cmpt, slct, wint, sbt, slctb, wintb, sbtb = _inproj_prompt(xp, g1, w_cat, w_kvt)
        kc = _compress_prompt(kcmp, pe2, wlo, whi)
        o_nsa = _nsa_prompt(qn, gn, kc, slctb, wintb)
        o_sb = _sb_prompt(qs, sbtb)
        xp = _post(xp.reshape(bp * t, D_MODEL), o_nsa.reshape(bp * t, D_NSA), o_sb.reshape(bp * t, D_SB),
                   gm.reshape(bp * t, D_MIX), wout_b, g2, g3, w1_b, w2_b, g4).reshape(bp, t, D_MODEL)
        outs["cmp_p"].append(cmpt)
        outs["slc_p"].append(slct)
        outs["win_p"].append(wint[:, :, t - min(WINDOW, t):])
        outs["sb_p"].append(sbt)

        qn, kcmp, kslc, kwin, gn, qs, ksb, gm = _inproj_sample(xs, g1, w_cat)
        kc = _compress_sample(c_cmp, page_table, lidx, pe2, wlo, whi)
        o_nsa, win_s = _nsa_sample(page_table, lidx, qn.reshape(bs, S_PAD, -1), gn.reshape(bs, S_PAD, -1), kc, c_slct,
                                   kslc.reshape(bs, S_PAD, KV_W), c_wint, kwin.reshape(bs, S_PAD, KV_W), win_s, s_new)
        o_sb = _sb_sample(page_table, lidx, qs.reshape(bs, S_PAD, D_SB), c_sbt, ksb.reshape(bs, S_PAD, 2 * D_SB))
        xs = _post(xs, o_nsa.reshape(bs * S_PAD, D_NSA), o_sb.reshape(bs * S_PAD, D_SB), gm, wout_b, g2, g3, w1_b,
                   w2_b, g4)
        outs["cmp_s"].append(kcmp.reshape(bs, S_PAD, 2, NSA_KV_HEADS, HEAD_DIM)[:, :s_new])
        outs["slc_s"].append(kslc.reshape(bs, S_PAD, 2, NSA_KV_HEADS, HEAD_DIM)[:, :s_new])
        outs["sb_s"].append(ksb.reshape(bs, S_PAD, 2, SB_HEADS, HEAD_DIM)[:, :s_new])

    y_s = xs.reshape(bs, S_PAD, D_MODEL)[:, :s_new]
    win_s = win_s.transpose(0, 1, 5, 2, 3, 4)
    return (xp, y_s,
            _rows_major(jnp.stack(outs["cmp_p"]), (depth, bp), NSA_KV_HEADS), jnp.stack(outs["cmp_s"]),
            _rows_major(jnp.stack(outs["slc_p"]), (depth, bp), NSA_KV_HEADS), jnp.stack(outs["slc_s"]),
            _rows_major(jnp.stack(outs["win_p"]), (depth, bp), NSA_KV_HEADS), win_s,
            _rows_major(jnp.stack(outs["sb_p"]), (depth, bp), SB_HEADS), jnp.stack(outs["sb_s"]))
```

```python
import functools

import numpy as np
import jax
import jax.numpy as jnp
from jax import lax
from jax.experimental import pallas as pl
from jax.experimental.pallas import tpu as pltpu

F32 = jnp.float32
BF16 = jnp.bfloat16

D_MODEL = 1024
HEAD_DIM = 64
NSA_HEADS = 8
NSA_KV_HEADS = 2
NSA_GROUP = NSA_HEADS // NSA_KV_HEADS
CMP_LEN = 32
CMP_STRIDE = 16
SEL_BLOCK = 64
SEL_TOP = 16
WINDOW = 512
SB_HEADS = 8
D_NSA = NSA_HEADS * HEAD_DIM
D_SB = SB_HEADS * HEAD_DIM
D_MIX = D_NSA + D_SB
D_FF = -(-8 * D_MODEL // (3 * 256)) * 256
KV_W = 2 * NSA_KV_HEADS * HEAD_DIM
PAGE_SIZE = 128
FORCE_BONUS = 1.0e4
NEG = -1.0e30
EPS = 1e-6
SLOPES = tuple(2.0 ** (-8.0 * (i + 1.0) / NSA_HEADS) for i in range(NSA_HEADS))

LANES = 128
S_PAD = 8
QB = 128
SEL_TILE = 512
SB_TQ = 256
SB_TK = 256
ROW_TILE = 256
VMEM_LIMIT = 56 * 1024 * 1024

C_QN = (0, 1024)
C_CMP = (1024, 1280)
C_SLC = (1280, 1536)
C_WIN = (1536, 1792)
C_GN = (1792, 1920)
C_QS = (1920, 2432)
C_SB = (2432, 3456)
C_GM = (3456, 4480)
W_CAT = 4480
R_CMP = (0, 256)
R_SLC = (256, 512)
R_WIN = (512, 768)
R_SB = (768, 1792)
W_KVT = 1792


def _dot(a, b):
    return jnp.dot(a, b, preferred_element_type=F32)


def _dot_nt(a, b):
    return lax.dot_general(a, b, (((1,), (1,)), ((), ())), preferred_element_type=F32)


def _split3(a):
    a1 = a.astype(BF16)
    r1 = a - a1.astype(F32)
    a2 = r1.astype(BF16)
    a3 = (r1 - a2.astype(F32)).astype(BF16)
    return a1, a2, a3


def _dot3_l(a, b):
    a1, a2, a3 = _split3(a)
    return _dot(a1, b) + _dot(a2, b) + _dot(a3, b)


def _dot3_r(a, b):
    b1, b2, b3 = _split3(b)
    return _dot(a, b1) + _dot(a, b2) + _dot(a, b3)


def _sigmoid(x):
    return 1.0 / (1.0 + jnp.exp(-x))


def _rms(x, g):
    return (x * lax.rsqrt(jnp.mean(x * x, axis=-1, keepdims=True) + EPS)) * g


def _masked_softmax(s, mask):
    s = jnp.where(mask, s, NEG)
    m = jnp.max(s, axis=-1, keepdims=True)
    p = jnp.where(mask, jnp.exp(s - m), 0.0)
    return p / jnp.maximum(jnp.sum(p, axis=-1, keepdims=True), 1e-30)


def _softmax_unnorm(s, mask):
    s = jnp.where(mask, s, 2.0 * NEG)
    m = jnp.maximum(jnp.max(s, axis=-1, keepdims=True), NEG)
    p = jnp.exp(s - m)
    return p, jnp.sum(p, axis=-1, keepdims=True)


def _topk_select(score, n_blk):
    n_pad = -(-n_blk // 8) * 8
    st = score.T[:n_pad, :]
    j = lax.broadcasted_iota(jnp.int32, (n_pad, LANES), 0)
    rank = jnp.zeros((n_pad, LANES), F32)
    for i in range(n_blk):
        row = jnp.broadcast_to(st[i:i + 1, :], (n_pad, LANES))
        beats = (row > st) | ((row == st) & (j > i))
        rank = rank + jnp.where(beats, 1.0, 0.0)
    sel_t = jnp.where(rank < float(SEL_TOP), 1.0, 0.0)
    if n_pad < LANES:
        sel_t = jnp.concatenate([sel_t, jnp.zeros((LANES - n_pad, LANES), F32)], axis=0)
    return sel_t.T


def _pair_heads(pieces, g):
    lane = lax.broadcasted_iota(jnp.int32, pieces[0].shape, 1)
    outs = []
    for m in range(2):
        a, b = pieces[2 * m], pieces[2 * m + 1]
        if g == 0:
            left, right = a, pltpu.roll(b, 64, 1)
        else:
            left, right = pltpu.roll(a, 64, 1), b
        outs.append(jnp.where(lane < 64, left, right))
    return jnp.concatenate(outs, axis=1)


def _pad_rows(x, rows):
    return jnp.concatenate([x, jnp.zeros((rows - x.shape[0], x.shape[1]), x.dtype)], axis=0)


def _inproj_sample_kernel(x_ref, g_ref, w_ref, qn_ref, cmp_ref, slc_ref, win_ref, gn_ref, qs_ref, sb_ref, gm_ref):
    hb = _rms(x_ref[...], g_ref[...]).astype(BF16)

    def proj(sec):
        return _dot(hb, w_ref[:, sec[0]:sec[1]])

    qn_ref[...] = (proj(C_QN) * (HEAD_DIM ** -0.5)).astype(BF16)
    cmp_ref[...] = proj(C_CMP)
    slc_ref[...] = proj(C_SLC)
    win_ref[...] = proj(C_WIN)
    gn_ref[...] = _sigmoid(proj(C_GN))
    qs_ref[...] = (proj(C_QS) * (HEAD_DIM ** -0.5)).astype(BF16)
    sb_ref[...] = proj(C_SB)
    gm_ref[...] = _sigmoid(proj(C_GM))


def _inproj_sample(x2d, g, w_cat):
    n = x2d.shape[0]
    tm = min(ROW_TILE, n)
    widths = [(1024, BF16), (256, F32), (256, F32), (256, F32), (128, F32), (512, BF16), (1024, F32), (1024, F32)]
    return pl.pallas_call(
        _inproj_sample_kernel,
        grid=(n // tm,),
        in_specs=[pl.BlockSpec((tm, D_MODEL), lambda i: (i, 0)),
                  pl.BlockSpec((1, D_MODEL), lambda i: (0, 0)),
                  pl.BlockSpec((D_MODEL, W_CAT), lambda i: (0, 0))],
        out_specs=[pl.BlockSpec((tm, w), lambda i: (i, 0)) for w, _ in widths],
        out_shape=[jax.ShapeDtypeStruct((n, w), dt) for w, dt in widths],
        compiler_params=pltpu.CompilerParams(dimension_semantics=("arbitrary",), vmem_limit_bytes=VMEM_LIMIT),
        name="inproj_sample",
    )(x2d, g, w_cat)


def _inproj_prompt_kernel(x_ref, g_ref, w_ref, wt_ref, qn_ref, gn_ref, qs_ref, gm_ref, cmp_ref,
                          cmpt_ref, slct_ref, wint_ref, sbt_ref, slctb_ref, wintb_ref, sbtb_ref):
    hb = _rms(x_ref[0], g_ref[...]).astype(BF16)

    def proj(sec):
        return _dot(hb, w_ref[:, sec[0]:sec[1]])

    def proj_t(sec):
        return _dot_nt(wt_ref[sec[0]:sec[1], :], hb)

    qn_ref[0] = (proj(C_QN) * (HEAD_DIM ** -0.5)).astype(BF16)
    gn_ref[0] = _sigmoid(proj(C_GN))
    qs_ref[0] = (proj(C_QS) * (HEAD_DIM ** -0.5)).astype(BF16)
    gm_ref[0] = _sigmoid(proj(C_GM))
    cmp_ref[0] = proj(C_CMP)
    cmpt_ref[0] = proj_t(R_CMP)
    u = proj_t(R_SLC)
    slct_ref[0] = u
    slctb_ref[0] = u.astype(BF16)
    u = proj_t(R_WIN)
    wint_ref[0] = u
    wintb_ref[0] = u.astype(BF16)
    u = proj_t(R_SB)
    sbt_ref[0] = u
    sbtb_ref[0] = u.astype(BF16)


def _inproj_prompt(x3d, g, w_cat, w_kvt):
    b, t, _ = x3d.shape
    tm = ROW_TILE
    row_outs = [(1024, BF16), (128, F32), (512, BF16), (1024, F32), (256, F32)]
    col_outs = [(256, F32), (256, F32), (256, F32), (1024, F32), (256, BF16), (256, BF16), (1024, BF16)]
    return pl.pallas_call(
        _inproj_prompt_kernel,
        grid=(b, t // tm),
        in_specs=[pl.BlockSpec((1, tm, D_MODEL), lambda i, j: (i, j, 0)),
                  pl.BlockSpec((1, D_MODEL), lambda i, j: (0, 0)),
                  pl.BlockSpec((D_MODEL, W_CAT), lambda i, j: (0, 0)),
                  pl.BlockSpec((W_KVT, D_MODEL), lambda i, j: (0, 0))],
        out_specs=([pl.BlockSpec((1, tm, w), lambda i, j: (i, j, 0)) for w, _ in row_outs]
                   + [pl.BlockSpec((1, w, tm), lambda i, j: (i, 0, j)) for w, _ in col_outs]),
        out_shape=([jax.ShapeDtypeStruct((b, t, w), dt) for w, dt in row_outs]
                   + [jax.ShapeDtypeStruct((b, w, t), dt) for w, dt in col_outs]),
        compiler_params=pltpu.CompilerParams(dimension_semantics=("arbitrary", "arbitrary"),
                                             vmem_limit_bytes=VMEM_LIMIT),
        name="inproj_prompt",
    )(x3d, g, w_cat, w_kvt)


def _compress_kernel(x_ref, pe_ref, wlo_ref, whi_ref, out_ref):
    x = x_ref[0]
    rows = x.shape[0]
    lo = _dot((x + pe_ref[0:1, :]).astype(BF16), wlo_ref[...])
    hi = _dot((x + pe_ref[1:2, :]).astype(BF16), whi_ref[...])
    out = lo + pltpu.roll(hi, rows - 1, 0)
    c = lax.broadcasted_iota(jnp.int32, out.shape, 0)
    out_ref[0] = jnp.where(c < rows - 1, out, 0.0)


def _compress_prompt(kv_cmp, pe2, wlo, whi):
    b, t, _ = kv_cmp.shape
    rows = t // CMP_STRIDE
    x = kv_cmp.reshape(b, rows, CMP_STRIDE * KV_W)
    return pl.pallas_call(
        _compress_kernel,
        grid=(b,),
        in_specs=[pl.BlockSpec((1, rows, CMP_STRIDE * KV_W), lambda i: (i, 0, 0)),
                  pl.BlockSpec((2, CMP_STRIDE * KV_W), lambda i: (0, 0)),
                  pl.BlockSpec((CMP_STRIDE * KV_W, KV_W), lambda i: (0, 0)),
                  pl.BlockSpec((CMP_STRIDE * KV_W, KV_W), lambda i: (0, 0))],
        out_specs=pl.BlockSpec((1, rows, KV_W), lambda i: (i, 0, 0)),
        out_shape=jax.ShapeDtypeStruct((b, rows, KV_W), F32),
        compiler_params=pltpu.CompilerParams(dimension_semantics=("arbitrary",), vmem_limit_bytes=VMEM_LIMIT),
        name="compress_prompt",
    )(x, pe2, wlo, whi)


def _compress_sample_kernel(*refs, n_pages):
    refs = refs[2:]
    page_refs = refs[:n_pages]
    pe_ref, wlo_ref, whi_ref, out_ref, x_scr = refs[n_pages:]
    for i, r in enumerate(page_refs):
        xt = r[0, 0].reshape(KV_W, PAGE_SIZE).T
        for half in range(KV_W // LANES):
            x_scr[half, i * PAGE_SIZE:(i + 1) * PAGE_SIZE, :] = xt[:, half * LANES:(half + 1) * LANES]
    rows = n_pages * PAGE_SIZE // CMP_STRIDE
    lo = jnp.zeros((rows, KV_W), F32)
    hi = jnp.zeros((rows, KV_W), F32)
    for p in range(CMP_STRIDE):
        cols = slice(p * KV_W, (p + 1) * KV_W)
        xp = jnp.concatenate([x_scr[half, pl.ds(p, rows, stride=CMP_STRIDE), :] for half in range(KV_W // LANES)],
                             axis=1)
        lo = lo + _dot((xp + pe_ref[0:1, cols]).astype(BF16), wlo_ref[cols, :])
        hi = hi + _dot((xp + pe_ref[1:2, cols]).astype(BF16), whi_ref[cols, :])
    out = lo + pltpu.roll(hi, rows - 1, 0)
    c = lax.broadcasted_iota(jnp.int32, out.shape, 0)
    out_ref[0] = jnp.where(c < rows - 1, out, 0.0)


def _compress_sample(cache_cmpt, page_table, lidx, pe2, wlo, whi):
    nb, n_pages = page_table.shape
    rows = n_pages * PAGE_SIZE // CMP_STRIDE
    kv_dims = (2, NSA_KV_HEADS, HEAD_DIM)

    def page_spec(i):
        return pl.BlockSpec((1, 1) + kv_dims + (PAGE_SIZE,), lambda b, pt, li: (li[0], pt[b, i], 0, 0, 0, 0))

    return pl.pallas_call(
        functools.partial(_compress_sample_kernel, n_pages=n_pages),
        grid_spec=pltpu.PrefetchScalarGridSpec(
            num_scalar_prefetch=2,
            grid=(nb,),
            in_specs=[page_spec(i) for i in range(n_pages)] + [
                pl.BlockSpec((2, CMP_STRIDE * KV_W), lambda b, pt, li: (0, 0)),
                pl.BlockSpec((CMP_STRIDE * KV_W, KV_W), lambda b, pt, li: (0, 0)),
                pl.BlockSpec((CMP_STRIDE * KV_W, KV_W), lambda b, pt, li: (0, 0))],
            out_specs=pl.BlockSpec((1, rows, KV_W), lambda b, pt, li: (b, 0, 0)),
            scratch_shapes=[pltpu.VMEM((KV_W // LANES, n_pages * PAGE_SIZE, LANES), F32)]),
        out_shape=jax.ShapeDtypeStruct((nb, rows, KV_W), F32),
        compiler_params=pltpu.CompilerParams(dimension_semantics=("arbitrary",), vmem_limit_bytes=VMEM_LIMIT),
        name="compress_sample",
    )(page_table, lidx, *([cache_cmpt] * n_pages), pe2, wlo, whi)


def _nsa_prompt_kernel(q_ref, gate_ref, kc_ref, slct_ref, wint_ref, m_ref, o_ref, m_scr, l_scr, acc_scr, *, n_sel):
    qb = pl.program_id(1)
    q0 = qb * QB
    q_all = q_ref[0]
    gates = gate_ref[0]
    kc = kc_ref[0]
    n_cmp_pad = kc.shape[0]
    kc2 = kc[:, :LANES].astype(BF16)
    vc2 = kc[:, LANES:].astype(BF16)
    qpos = q0 + lax.broadcasted_iota(jnp.int32, (QB, 1), 0)
    c_end = lax.broadcasted_iota(jnp.int32, (1, n_cmp_pad), 1) * CMP_STRIDE + (CMP_LEN - 1)
    dist_c = (qpos - c_end).astype(F32)
    mask_c = dist_c >= 0.0
    blk = lax.broadcasted_iota(jnp.int32, (QB, LANES), 1)
    cur = qpos >> 6
    allowed = (blk * SEL_BLOCK <= qpos) & (blk < n_sel)
    forced = (blk == 0) | (blk == cur) | (blk == cur - 1)
    w_start = pl.multiple_of(jnp.maximum(q0 - WINDOW, 0), QB)
    n_win = WINDOW + QB
    n_tiles = (q0 + QB + SEL_TILE - 1) // SEL_TILE

    n_rows = NSA_GROUP * QB

    qms, sels, o_cs = [], [], []
    for g in range(NSA_KV_HEADS):
        heads = [g * NSA_GROUP + r for r in range(NSA_GROUP)]
        qm = jnp.concatenate([q_all[:, h * LANES:(h + 1) * LANES] for h in heads], axis=0)
        s = _dot_nt(qm, kc2)
        p_sum = jnp.zeros((QB, n_cmp_pad), F32)
        p_rows = []
        for r, h in enumerate(heads):
            p, l = _softmax_unnorm(s[r * QB:(r + 1) * QB] - SLOPES[h] * dist_c, mask_c)
            p = p * (1.0 / jnp.maximum(l, 1e-30))
            p_sum = p_sum + p
            p_rows.append(p.astype(BF16))
        o_cs.append(_dot(jnp.concatenate(p_rows, axis=0), vc2))
        imp = _dot3_l(p_sum, m_ref[...])
        score = jnp.where(allowed, imp + jnp.where(forced, FORCE_BONUS, 0.0), -FORCE_BONUS)
        score = jnp.where(blk < n_sel, score, -3.0 * FORCE_BONUS)
        sels.append(jnp.where(allowed, _topk_select(score, n_sel), 0.0).astype(BF16))
        qms.append(qm)

    m_scr[...] = jnp.full(m_scr.shape, NEG, F32)
    l_scr[...] = jnp.zeros(l_scr.shape, F32)
    acc_scr[...] = jnp.zeros(acc_scr.shape, F32)

    def sel_tile(kt, carry):
        k0 = pl.multiple_of(kt * SEL_TILE, SEL_TILE)
        k2t = slct_ref[0, 0:LANES, pl.ds(k0, SEL_TILE)]
        v2t = slct_ref[0, LANES:2 * LANES, pl.ds(k0, SEL_TILE)]
        kpos = k0 + lax.broadcasted_iota(jnp.int32, (1, SEL_TILE), 1)
        dist = (qpos - kpos).astype(F32)
        e_row = lax.broadcasted_iota(jnp.int32, (LANES, SEL_TILE), 0)
        e_col = lax.broadcasted_iota(jnp.int32, (LANES, SEL_TILE), 1)
        expand = jnp.where(e_row == kt * (SEL_TILE // SEL_BLOCK) + (e_col >> 6), 1.0, 0.0).astype(BF16)
        for g in range(NSA_KV_HEADS):
            s = _dot(qms[g], k2t)
            mask = (_dot(sels[g], expand) > 0.5) & (dist >= 0.0)
            p_rows = []
            for r in range(NSA_GROUP):
                rows = slice(g * n_rows + r * QB, g * n_rows + (r + 1) * QB)
                sr = jnp.where(mask, s[r * QB:(r + 1) * QB] - SLOPES[g * NSA_GROUP + r] * dist, 2.0 * NEG)
                m_old = m_scr[rows]
                m_new = jnp.maximum(m_old, jnp.max(sr, axis=-1, keepdims=True))
                alpha = jnp.exp(m_old - m_new)
                p = jnp.exp(sr - m_new)
                l_scr[rows] = alpha * l_scr[rows] + jnp.sum(p, axis=-1, keepdims=True)
                m_scr[rows] = m_new
                acc_scr[rows] = alpha * acc_scr[rows]
                p_rows.append(p.astype(BF16))
            acc_scr[g * n_rows:(g + 1) * n_rows] += _dot_nt(jnp.concatenate(p_rows, axis=0), v2t)
        return carry

    lax.fori_loop(0, n_tiles, sel_tile, 0)
    o_s_all = acc_scr[...] * (1.0 / jnp.maximum(l_scr[...], 1e-30))

    kwt = wint_ref[0, :, pl.ds(w_start, n_win)]
    kpos = w_start + lax.broadcasted_iota(jnp.int32, (1, n_win), 1)
    dist_i = qpos - kpos
    mask_w = (dist_i >= 0) & (dist_i < WINDOW)
    dist_w = dist_i.astype(F32)
    for g in range(NSA_KV_HEADS):
        heads = [g * NSA_GROUP + r for r in range(NSA_GROUP)]
        s = _dot(qms[g], kwt[:LANES])
        p_rows, l_rows = [], []
        for r, h in enumerate(heads):
            p, l = _softmax_unnorm(s[r * QB:(r + 1) * QB] - SLOPES[h] * dist_w, mask_w)
            p_rows.append(p.astype(BF16))
            l_rows.append(l)
        o_w = _dot_nt(jnp.concatenate(p_rows, axis=0), kwt[LANES:])
        o_w = o_w * (1.0 / jnp.maximum(jnp.concatenate(l_rows, axis=0), 1e-30))
        o_c = o_cs[g]
        o_s = o_s_all[g * n_rows:(g + 1) * n_rows]

        pieces = []
        for r, h in enumerate(heads):
            rows = slice(r * QB, (r + 1) * QB)
            pieces.append(gates[:, 3 * h:3 * h + 1] * o_c[rows] + gates[:, 3 * h + 1:3 * h + 2] * o_s[rows]
                          + gates[:, 3 * h + 2:3 * h + 3] * o_w[rows])
        o_ref[0, :, g * 256:(g + 1) * 256] = _pair_heads(pieces, g)


def _sel_weights(n_cmp_pad, n_cmp, n_sel):
    ratio = SEL_BLOCK // CMP_STRIDE
    m = np.zeros((n_cmp_pad, LANES), np.float32)
    for j in range(n_sel):
        for off in range(-1, ratio):
            c = ratio * j + off
            if 0 <= c < n_cmp:
                m[c, j] += 1.0 if off in (-1, ratio - 1) else 2.0
    return jnp.asarray(m, BF16)


def _nsa_prompt(qn, gn, kc, slctb, wintb):
    b, t, _ = qn.shape
    n_sel = -(-t // SEL_BLOCK)
    n_cmp_pad = kc.shape[1]
    m_mat = _sel_weights(n_cmp_pad, t // CMP_STRIDE - 1, n_sel)
    return pl.pallas_call(
        functools.partial(_nsa_prompt_kernel, n_sel=n_sel),
        grid=(b, t // QB),
        in_specs=[pl.BlockSpec((1, QB, NSA_HEADS * LANES), lambda i, j: (i, j, 0)),
                  pl.BlockSpec((1, QB, LANES), lambda i, j: (i, j, 0)),
                  pl.BlockSpec((1, n_cmp_pad, KV_W), lambda i, j: (i, 0, 0)),
                  pl.BlockSpec((1, KV_W, t), lambda i, j: (i, 0, 0)),
                  pl.BlockSpec((1, KV_W, t), lambda i, j: (i, 0, 0)),
                  pl.BlockSpec((n_cmp_pad, LANES), lambda i, j: (0, 0))],
        out_specs=pl.BlockSpec((1, QB, D_NSA), lambda i, j: (i, j, 0)),
        out_shape=jax.ShapeDtypeStruct((b, t, D_NSA), F32),
        scratch_shapes=[pltpu.VMEM((NSA_HEADS * QB, 1), F32), pltpu.VMEM((NSA_HEADS * QB, 1), F32),
                        pltpu.VMEM((NSA_HEADS * QB, LANES), F32)],
        compiler_params=pltpu.CompilerParams(dimension_semantics=("arbitrary", "arbitrary"),
                                             vmem_limit_bytes=VMEM_LIMIT),
        name="nsa_prompt",
    )(qn, gn, kc, slctb, wintb, m_mat)


def _softplus(z):
    neg_abs = lax.bitcast_convert_type(lax.bitcast_convert_type(z, jnp.int32) | jnp.int32(-2 ** 31), F32)
    return jnp.maximum(z, 0.0) + jnp.log(1.0 + jnp.exp(neg_abs))


def _suffix_sums(x, u2):
    hi = x.astype(BF16)
    lo = (x - hi.astype(F32)).astype(BF16)
    return _dot(jnp.concatenate([hi, lo], axis=1), u2)


def _sb_prompt_kernel(q_ref, kt_ref, vt_ref, u_ref, o_ref, c_scr, acc_scr, z_scr, a_scr):
    qb = pl.program_id(2)
    q2 = q_ref[0]
    lane = lax.broadcasted_iota(jnp.int32, q2.shape, 1)
    qm = jnp.concatenate([jnp.where((lane >> 6) == h, q2, jnp.zeros_like(q2)) for h in range(2)], axis=0)
    c_scr[...] = jnp.zeros(c_scr.shape, F32)
    acc_scr[...] = jnp.zeros(acc_scr.shape, F32)
    u2 = u_ref[...]

    def scores(kt):
        k0 = pl.multiple_of(kt * SB_TK, SB_TK)
        return _dot(qm, kt_ref[0, :, pl.ds(k0, SB_TK)])

    def add_values(kt):
        k0 = pl.multiple_of(kt * SB_TK, SB_TK)
        acc_scr[...] += _dot_nt(a_scr[...], vt_ref[0, :, pl.ds(k0, SB_TK)])

    def weights(z, mask):
        sp = _softplus(z)
        if mask is not None:
            sp = jnp.where(mask, sp, 0.0)
        a = jnp.exp((z - sp) - (c_scr[...] + _suffix_sums(sp, u2)))
        if mask is not None:
            a = jnp.where(mask, a, 0.0)
        c_scr[...] += jnp.sum(sp, axis=-1, keepdims=True)
        return a.astype(BF16)

    row = lax.broadcasted_iota(jnp.int32, (2 * SB_TQ, SB_TK), 0) & (SB_TQ - 1)
    col = lax.broadcasted_iota(jnp.int32, (2 * SB_TQ, SB_TK), 1)
    z_scr[...] = scores(jnp.maximum(qb - 1, 0))
    a_scr[...] = weights(scores(qb), col < row)

    def earlier(i, carry):
        kt = qb - 1 - i
        z = z_scr[...]
        add_values(kt + 1)
        z_scr[...] = scores(jnp.maximum(kt - 1, 0))
        a_scr[...] = weights(z, None)
        return carry

    lax.fori_loop(0, qb, earlier, 0)
    add_values(0)
    lane_o = lax.broadcasted_iota(jnp.int32, (SB_TQ, LANES), 1)
    o_ref[0] = jnp.where(lane_o < HEAD_DIM, acc_scr[0:SB_TQ], acc_scr[SB_TQ:])


def _later_matrix(n):
    u = np.tril(np.ones((n, n), np.float32), -1)
    return jnp.asarray(np.concatenate([u, u], axis=0), BF16)


def _sb_prompt(qs, sbtb):
    b, t, _ = qs.shape
    n_hp = SB_HEADS // 2
    return pl.pallas_call(
        _sb_prompt_kernel,
        grid=(b, n_hp, t // SB_TQ),
        in_specs=[pl.BlockSpec((1, SB_TQ, LANES), lambda i, h, j: (i, j, h)),
                  pl.BlockSpec((1, LANES, t), lambda i, h, j: (i, h, 0)),
                  pl.BlockSpec((1, LANES, t), lambda i, h, j: (i, n_hp + h, 0)),
                  pl.BlockSpec((2 * SB_TK, SB_TK), lambda i, h, j: (0, 0))],
        out_specs=pl.BlockSpec((1, SB_TQ, LANES), lambda i, h, j: (i, j, h)),
        out_shape=jax.ShapeDtypeStruct((b, t, D_SB), F32),
        scratch_shapes=[pltpu.VMEM((2 * SB_TQ, 1), F32), pltpu.VMEM((2 * SB_TQ, LANES), F32),
                        pltpu.VMEM((2 * SB_TQ, SB_TK), F32), pltpu.VMEM((2 * SB_TQ, SB_TK), BF16)],
        compiler_params=pltpu.CompilerParams(dimension_semantics=("arbitrary", "arbitrary", "arbitrary"),
                                             vmem_limit_bytes=VMEM_LIMIT),
        name="sb_prompt",
    )(qs, sbtb, sbtb, _later_matrix(SB_TK))


def _nsa_sample_kernel(*refs, n_pages, past_len, s_new, n_sel, aliased):
    refs = refs[2:]
    qn_ref, gate_ref, kc_ref = refs[:3]
    slc_pages = refs[3:3 + n_pages]
    refs = refs[3 + n_pages:]
    slcnew_ref, wcache_ref, wnew_ref, m_ref, slope_ref, gsum_ref, expand_ref = refs[:7]
    refs = refs[7:]
    if aliased:
        refs = refs[1:]
    onsa_ref, wout_ref = refs

    rows = NSA_HEADS * S_PAD
    row = lax.broadcasted_iota(jnp.int32, (rows, 1), 0)
    step = row & (S_PAD - 1)
    qpos = past_len + step
    slope = slope_ref[...]
    gates = gate_ref[0]
    q_all = qn_ref[0]
    qm = jnp.concatenate([q_all[:, h * LANES:(h + 1) * LANES] for h in range(NSA_HEADS)], axis=0)
    lane = lax.broadcasted_iota(jnp.int32, (rows, LANES), 1)

    kc = kc_ref[0]
    n_cmp_pad = kc.shape[0]
    c_end = lax.broadcasted_iota(jnp.int32, (1, n_cmp_pad), 1) * CMP_STRIDE + (CMP_LEN - 1)
    dist_c = (qpos - c_end).astype(F32)
    p_c = _masked_softmax(_dot_nt(qm, kc[:, :LANES].astype(BF16)) - slope * dist_c, dist_c >= 0.0)
    o_c = _dot(p_c.astype(BF16), kc[:, LANES:].astype(BF16))

    p_sum = _dot3_r(gsum_ref[...], p_c)
    imp = _dot3_l(p_sum, m_ref[...])
    cur = qpos >> 6
    allowed = (lane * SEL_BLOCK <= qpos) & (lane < n_sel)
    forced = (lane == 0) | (lane == cur) | (lane == cur - 1)
    score = jnp.where(allowed, imp + jnp.where(forced, FORCE_BONUS, 0.0), -FORCE_BONUS)
    score = jnp.where(lane < n_sel, score, -3.0 * FORCE_BONUS)
    sel = _topk_select(_pad_rows(score, LANES), n_sel)[:rows]
    sel = jnp.where(allowed, sel, 0.0)

    k2t = jnp.concatenate([r[0, 0, 0].reshape(LANES, PAGE_SIZE).astype(BF16) for r in slc_pages], axis=1)
    v2t = jnp.concatenate([r[0, 0, 1].reshape(LANES, PAGE_SIZE).astype(BF16) for r in slc_pages], axis=1)
    new = _pad_rows(slcnew_ref[0], LANES).astype(BF16)
    s_past = _dot(qm, k2t)
    s_nw = _dot_nt(qm, new[:, :LANES])
    kpos = lax.broadcasted_iota(jnp.int32, (1, past_len), 1)
    dist_p = (qpos - kpos).astype(F32)
    mask_p = (_dot(sel.astype(BF16), expand_ref[...]) > 0.5) & (dist_p >= 0.0)
    dist_n = (step - lane).astype(F32)
    last_blk = past_len // SEL_BLOCK
    sel_last = jnp.sum(jnp.where(lane == last_blk, sel, 0.0), axis=-1, keepdims=True)
    mask_n = (dist_n >= 0.0) & (sel_last > 0.5) & (lane < S_PAD)
    s_past = jnp.where(mask_p, s_past - slope * dist_p, NEG)
    s_nw = jnp.where(mask_n, s_nw - slope * dist_n, NEG)
    m = jnp.maximum(jnp.max(s_past, axis=-1, keepdims=True), jnp.max(s_nw, axis=-1, keepdims=True))
    p_past = jnp.where(mask_p, jnp.exp(s_past - m), 0.0)
    p_new = jnp.where(mask_n, jnp.exp(s_nw - m), 0.0)
    denom = jnp.sum(p_past, axis=-1, keepdims=True) + jnp.sum(p_new, axis=-1, keepdims=True)
    o_s = (_dot_nt(p_past.astype(BF16), v2t) + _dot(p_new.astype(BF16), new[:, LANES:])) / jnp.maximum(denom, 1e-30)

    wc = wcache_ref[0, 0]
    win_buf = wc.shape[-1]
    wct = wc.reshape(KV_W, win_buf)
    wnew = wnew_ref[0]
    wnew_pad = _pad_rows(wnew, LANES)
    shifted = pltpu.roll(wct, win_buf - s_new, 1)
    tail = pltpu.roll(wnew_pad.T, LANES - s_new, 1)
    lane_w = lax.broadcasted_iota(jnp.int32, (KV_W, LANES), 1)
    last = jnp.where(lane_w >= LANES - s_new, tail, shifted[:, win_buf - LANES:])
    wout_ref[0, 0] = jnp.concatenate([shifted[:, :win_buf - LANES], last], axis=1).reshape(wc.shape)
    wcb = wct.astype(BF16)
    wnb = wnew_pad.astype(BF16)
    kpos_w = (past_len - win_buf) + lax.broadcasted_iota(jnp.int32, (1, win_buf), 1)
    dist_wi = qpos - kpos_w
    mask_wp = (dist_wi >= 0) & (dist_wi < WINDOW)
    mask_wn = (dist_n >= 0.0) & (lane < S_PAD)
    s_wp = jnp.where(mask_wp, _dot(qm, wcb[:LANES]) - slope * dist_wi.astype(F32), NEG)
    s_wn = jnp.where(mask_wn, _dot_nt(qm, wnb[:, :LANES]) - slope * dist_n, NEG)
    m = jnp.maximum(jnp.max(s_wp, axis=-1, keepdims=True), jnp.max(s_wn, axis=-1, keepdims=True))
    p_wp = jnp.where(mask_wp, jnp.exp(s_wp - m), 0.0)
    p_wn = jnp.where(mask_wn, jnp.exp(s_wn - m), 0.0)
    denom = jnp.sum(p_wp, axis=-1, keepdims=True) + jnp.sum(p_wn, axis=-1, keepdims=True)
    o_w = (_dot_nt(p_wp.astype(BF16), wcb[LANES:]) + _dot(p_wn.astype(BF16), wnb[:, LANES:])) / jnp.maximum(denom, 1e-30)

    for g in range(NSA_KV_HEADS):
        pieces = []
        for r in range(NSA_GROUP):
            h = g * NSA_GROUP + r
            rs = slice(h * S_PAD, (h + 1) * S_PAD)
            pieces.append(gates[:, 3 * h:3 * h + 1] * o_c[rs] + gates[:, 3 * h + 1:3 * h + 2] * o_s[rs]
                          + gates[:, 3 * h + 2:3 * h + 3] * o_w[rs])
        onsa_ref[0, :, g * 256:(g + 1) * 256] = _pair_heads(pieces, g)


def _nsa_sample(page_table, lidx, qn, gn, kc, cache_slct, slc_new, cache_wint, win_new, win_buf_out, s_new):
    nb, n_pages = page_table.shape
    past_len = n_pages * PAGE_SIZE
    assert s_new < CMP_STRIDE and s_new <= S_PAD and past_len % SEL_BLOCK == 0
    n_sel = -(-(past_len + s_new) // SEL_BLOCK)
    n_cmp_pad = kc.shape[1]
    depth, _, _, _, _, win_buf = cache_wint.shape
    rows = NSA_HEADS * S_PAD
    m_mat = _sel_weights(n_cmp_pad, (past_len + s_new) // CMP_STRIDE - 1, n_sel)
    slope = jnp.asarray(np.repeat(np.asarray(SLOPES, np.float32), S_PAD).reshape(rows, 1))
    rr = np.arange(rows)
    gsum = jnp.asarray(((rr[:, None] // (S_PAD * NSA_GROUP) == rr[None, :] // (S_PAD * NSA_GROUP))
                        & (rr[:, None] % S_PAD == rr[None, :] % S_PAD)).astype(np.float32), BF16)
    expand = jnp.asarray((np.arange(LANES)[:, None] == np.arange(past_len)[None, :] // SEL_BLOCK).astype(np.float32),
                         BF16)
    aliased = win_buf_out is not None
    kv_dims = (2, NSA_KV_HEADS, HEAD_DIM)

    def const(shape):
        return pl.BlockSpec(shape, lambda b, pt, li: tuple(0 for _ in shape))

    def per_req(shape):
        return pl.BlockSpec((1,) + shape, lambda b, pt, li: (b,) + tuple(0 for _ in shape))

    def page_spec(i):
        return pl.BlockSpec((1, 1) + kv_dims + (PAGE_SIZE,), lambda b, pt, li: (li[0], pt[b, i], 0, 0, 0, 0))

    win_spec = pl.BlockSpec((1, 1) + kv_dims + (win_buf,), lambda b, pt, li: (li[0], b, 0, 0, 0, 0))
    in_specs = ([per_req((S_PAD, NSA_HEADS * LANES)), per_req((S_PAD, LANES)), per_req((n_cmp_pad, KV_W))]
                + [page_spec(i) for i in range(n_pages)]
                + [per_req((S_PAD, KV_W)), win_spec, per_req((S_PAD, KV_W)),
                   const((n_cmp_pad, LANES)), const((rows, 1)), const((rows, rows)), const((LANES, past_len))])
    args = [qn, gn, kc] + [cache_slct] * n_pages + [slc_new, cache_wint, win_new, m_mat, slope, gsum, expand]
    aliases = {}
    if aliased:
        in_specs.append(pl.BlockSpec(memory_space=pl.ANY))
        args.append(win_buf_out)
        aliases = {2 + len(args) - 1: 1}
    return pl.pallas_call(
        functools.partial(_nsa_sample_kernel, n_pages=n_pages, past_len=past_len, s_new=s_new, n_sel=n_sel,
                          aliased=aliased),
        grid_spec=pltpu.PrefetchScalarGridSpec(
            num_scalar_prefetch=2,
            grid=(nb,),
            in_specs=in_specs,
            out_specs=[per_req((S_PAD, D_NSA)), win_spec]),
        out_shape=[jax.ShapeDtypeStruct((nb, S_PAD, D_NSA), F32),
                   jax.ShapeDtypeStruct((depth, nb) + kv_dims + (win_buf,), F32)],
        input_output_aliases=aliases,
        compiler_params=pltpu.CompilerParams(dimension_semantics=("arbitrary",), vmem_limit_bytes=VMEM_LIMIT),
        name="nsa_sample",
    )(page_table, lidx, *args)


def _sb_sample_kernel(*refs, n_pages):
    refs = refs[2:]
    qs_ref = refs[0]
    sb_pages = refs[1:1 + n_pages]
    sbnew_ref, u_ref, osb_ref = refs[1 + n_pages:]

    rows = SB_HEADS * S_PAD
    step = lax.broadcasted_iota(jnp.int32, (rows, 1), 0) & (S_PAD - 1)
    lane = lax.broadcasted_iota(jnp.int32, (rows, LANES), 1)
    q_sb = qs_ref[0]
    q_rep = jnp.concatenate([q_sb] * SB_HEADS, axis=0)
    lane_sb = lax.broadcasted_iota(jnp.int32, q_rep.shape, 1)
    row_sb = lax.broadcasted_iota(jnp.int32, q_rep.shape, 0)
    q_bd = jnp.where((lane_sb >> 6) == (row_sb >> 3), q_rep, jnp.zeros_like(q_rep))
    kt = jnp.concatenate([r[0, 0, 0].reshape(D_SB, PAGE_SIZE).astype(BF16) for r in sb_pages], axis=1)
    vt = jnp.concatenate([r[0, 0, 1].reshape(D_SB, PAGE_SIZE).astype(BF16) for r in sb_pages], axis=1)
    sbn = _pad_rows(sbnew_ref[0], LANES).astype(BF16)
    z_past = _dot(q_bd, kt)
    z_new = _dot_nt(q_bd, sbn[:, :D_SB])
    u = u_ref[...]
    mask_sn = lane < step
    sp = jnp.where(mask_sn, _softplus(z_new), 0.0)
    a_new = jnp.where(mask_sn, jnp.exp((z_new - sp) - _suffix_sums(sp, u)), 0.0)
    carry = jnp.sum(sp, axis=-1, keepdims=True)
    a_tiles = [None] * n_pages
    for t in range(n_pages - 1, -1, -1):
        z = z_past[:, t * LANES:(t + 1) * LANES]
        sp = _softplus(z)
        a_tiles[t] = jnp.exp((z - sp) - (carry + _suffix_sums(sp, u))).astype(BF16)
        carry = carry + jnp.sum(sp, axis=-1, keepdims=True)
    o_all = _dot_nt(jnp.concatenate(a_tiles, axis=1), vt) + _dot(a_new.astype(BF16), sbn[:, D_SB:])
    lane_o = lax.broadcasted_iota(jnp.int32, (S_PAD, D_SB), 1)
    o_sb = jnp.zeros((S_PAD, D_SB), F32)
    for h in range(SB_HEADS):
        o_sb = o_sb + jnp.where((lane_o >> 6) == h, o_all[h * S_PAD:(h + 1) * S_PAD], 0.0)
    osb_ref[0] = o_sb


def _sb_sample(page_table, lidx, qs, cache_sbt, sb_new):
    nb, n_pages = page_table.shape
    kv_dims = (2, SB_HEADS, HEAD_DIM)

    def per_req(shape):
        return pl.BlockSpec((1,) + shape, lambda b, pt, li: (b,) + tuple(0 for _ in shape))

    def page_spec(i):
        return pl.BlockSpec((1, 1) + kv_dims + (PAGE_SIZE,), lambda b, pt, li: (li[0], pt[b, i], 0, 0, 0, 0))

    in_specs = ([per_req((S_PAD, D_SB))] + [page_spec(i) for i in range(n_pages)]
                + [per_req((S_PAD, 2 * D_SB)), pl.BlockSpec((2 * LANES, LANES), lambda b, pt, li: (0, 0))])
    return pl.pallas_call(
        functools.partial(_sb_sample_kernel, n_pages=n_pages),
        grid_spec=pltpu.PrefetchScalarGridSpec(
            num_scalar_prefetch=2,
            grid=(nb,),
            in_specs=in_specs,
            out_specs=per_req((S_PAD, D_SB))),
        out_shape=jax.ShapeDtypeStruct((nb, S_PAD, D_SB), F32),
        compiler_params=pltpu.CompilerParams(dimension_semantics=("arbitrary",), vmem_limit_bytes=VMEM_LIMIT),
        name="sb_sample",
    )(page_table, lidx, qs, *([cache_sbt] * n_pages), sb_new, _later_matrix(LANES))


def _post_kernel(x_ref, onsa_ref, osb_ref, gm_ref, wout_ref, gpost_ref, gpre2_ref, w1_ref, w2_ref, gpost2_ref, y_ref):
    gm = gm_ref[...]
    y = (_dot((gm[:, :D_NSA] * onsa_ref[...]).astype(BF16), wout_ref[0:D_NSA, :])
         + _dot((gm[:, D_NSA:] * osb_ref[...]).astype(BF16), wout_ref[D_NSA:, :]))
    x = x_ref[...] + _rms(y, gpost_ref[...])
    hb = _rms(x, gpre2_ref[...]).astype(BF16)
    a = _dot(hb, w1_ref[:, 0:D_FF])
    b = _dot(hb, w1_ref[:, D_FF:])
    act = (a * _sigmoid(a)) * b
    y_ref[...] = x + _rms(_dot(act.astype(BF16), w2_ref[...]), gpost2_ref[...])


def _post(x2d, onsa, osb, gm, wout, gpost, gpre2, w1, w2, gpost2):
    n = x2d.shape[0]
    tm = min(ROW_TILE, n)

    def rows(w):
        return pl.BlockSpec((tm, w), lambda i: (i, 0))

    def whole(shape):
        return pl.BlockSpec(shape, lambda i: (0, 0), pipeline_mode=pl.Buffered(1))

    return pl.pallas_call(
        _post_kernel,
        grid=(n // tm,),
        in_specs=[rows(D_MODEL), rows(D_NSA), rows(D_SB), rows(D_MIX), whole((D_MIX, D_MODEL)), whole((1, D_MODEL)),
                  whole((1, D_MODEL)), whole((D_MODEL, 2 * D_FF)), whole((D_FF, D_MODEL)), whole((1, D_MODEL))],
        out_specs=rows(D_MODEL),
        out_shape=jax.ShapeDtypeStruct((n, D_MODEL), F32),
        compiler_params=pltpu.CompilerParams(dimension_semantics=("arbitrary",), vmem_limit_bytes=VMEM_LIMIT),
        name="post",
    )(x2d, onsa, osb, gm, wout, gpost, gpre2, w1, w2, gpost2)


def _relayout_w_in(w):
    z64 = jnp.zeros((D_MODEL, HEAD_DIM), w.dtype)
    cols = []
    for h in range(NSA_HEADS):
        wh = w[:, h * HEAD_DIM:(h + 1) * HEAD_DIM]
        cols += [wh, z64] if h // NSA_GROUP == 0 else [z64, wh]
    off = D_NSA
    cols.append(w[:, off:off + 3 * KV_W])
    off += 3 * KV_W
    cols.append(w[:, off:off + 3 * NSA_HEADS])
    cols.append(jnp.zeros((D_MODEL, LANES - 3 * NSA_HEADS), w.dtype))
    off += 3 * NSA_HEADS
    cols.append(w[:, off:])
    w_cat = jnp.concatenate(cols, axis=1).astype(BF16)
    w_kvt = jnp.concatenate([w_cat[:, C_CMP[0]:C_WIN[1]], w_cat[:, C_SB[0]:C_SB[1]]], axis=1).T
    return w_cat, w_kvt


def _relayout_cmp(pe, w):
    eye_k = jnp.eye(2, dtype=w.dtype)
    eye_g = jnp.eye(NSA_KV_HEADS, dtype=w.dtype)
    outs = []
    for half in range(2):
        wh = w[:, half * CMP_STRIDE:(half + 1) * CMP_STRIDE]
        big = jnp.einsum('kpde,kK,gG->pkgdKGe', wh, eye_k, eye_g)
        outs.append(big.reshape(CMP_STRIDE * KV_W, KV_W).astype(BF16))
    pes = []
    for half in range(2):
        ph = pe[:, half * CMP_STRIDE:(half + 1) * CMP_STRIDE]
        ph = jnp.broadcast_to(ph.transpose(1, 0, 2)[:, :, None, :], (CMP_STRIDE, 2, NSA_KV_HEADS, HEAD_DIM))
        pes.append(ph.reshape(1, CMP_STRIDE * KV_W))
    return jnp.concatenate(pes, axis=0), outs[0], outs[1]


def _rows_major(xt, lead, n_heads):
    t = xt.shape[-1]
    k = len(lead)
    xt = xt.reshape(lead + (2, n_heads, HEAD_DIM, t))
    return xt.transpose(tuple(range(k)) + (k + 3, k, k + 1, k + 2))


def kernel(x_prompt, x_sample, cache_cmp_kv, cache_slc_kv, cache_win_kv, cache_sb_kv, page_table, g_pre_mix,
           g_post_mix, g_pre_ffn, g_post_ffn, w_in, pe_cmp, w_cmp, w_out, w_ffn_in, w_ffn_out):
    depth = w_in.shape[0]
    bp, t, _ = x_prompt.shape
    bs, s_new, _ = x_sample.shape
    c_cmpt = cache_cmp_kv.transpose(0, 1, 3, 4, 5, 2)
    c_slct = cache_slc_kv.transpose(0, 1, 3, 4, 5, 2)
    c_wint = cache_win_kv.transpose(0, 1, 3, 4, 5, 2)
    c_sbt = cache_sb_kv.transpose(0, 1, 3, 4, 5, 2)

    xp = x_prompt
    xs = jnp.pad(x_sample, ((0, 0), (0, S_PAD - s_new), (0, 0))).reshape(bs * S_PAD, D_MODEL)
    outs = {k: [] for k in ("cmp_p", "cmp_s", "slc_p", "slc_s", "win_p", "sb_p", "sb_s")}
    win_s = None
    for l in range(depth):
        w_cat, w_kvt = _relayout_w_in(w_in[l])
        pe2, wlo, whi = _relayout_cmp(pe_cmp[l], w_cmp[l])
        wout_b = w_out[l].astype(BF16)
        w1_b = w_ffn_in[l].astype(BF16)
        w2_b = w_ffn_out[l].astype(BF16)
        g1 = g_pre_mix[l].reshape(1, D_MODEL)
        g2 = g_post_mix[l].reshape(1, D_MODEL)
        g3 = g_pre_ffn[l].reshape(1, D_MODEL)
        g4 = g_post_ffn[l].reshape(1, D_MODEL)
        lidx = jnp.full((1,), l, jnp.int32)

        qn, gn, qs, gm, kcmp, cmpt, slct, wint, sbt, slctb, wintb, sbtb = _inproj_prompt(xp, g1, w_cat, w_kvt)
        kc = _compress_prompt(kcmp, pe2, wlo, whi)
        o_nsa = _nsa_prompt(qn, gn, kc, slctb, wintb)
        o_sb = _sb_prompt(qs, sbtb)
        xp = _post(xp.reshape(bp * t, D_MODEL), o_nsa.reshape(bp * t, D_NSA), o_sb.reshape(bp * t, D_SB),
                   gm.reshape(bp * t, D_MIX), wout_b, g2, g3, w1_b, w2_b, g4).reshape(bp, t, D_MODEL)
        outs["cmp_p"].append(cmpt)
        outs["slc_p"].append(slct)
        outs["win_p"].append(wint[:, :, t - min(WINDOW, t):])
        outs["sb_p"].append(sbt)

        qn, kcmp, kslc, kwin, gn, qs, ksb, gm = _inproj_sample(xs, g1, w_cat)
        kc = _compress_sample(c_cmpt, page_table, lidx, pe2, wlo, whi)
        o_nsa, win_s = _nsa_sample(page_table, lidx, qn.reshape(bs, S_PAD, -1), gn.reshape(bs, S_PAD, -1), kc, c_slct,
                                   kslc.reshape(bs, S_PAD, KV_W), c_wint, kwin.reshape(bs, S_PAD, KV_W), win_s, s_new)
        o_sb = _sb_sample(page_table, lidx, qs.reshape(bs, S_PAD, D_SB), c_sbt, ksb.reshape(bs, S_PAD, 2 * D_SB))
        xs = _post(xs, o_nsa.reshape(bs * S_PAD, D_NSA), o_sb.reshape(bs * S_PAD, D_SB), gm, wout_b, g2, g3, w1_b,
                   w2_b, g4)
        outs["cmp_s"].append(kcmp.reshape(bs, S_PAD, 2, NSA_KV_HEADS, HEAD_DIM)[:, :s_new])
        outs["slc_s"].append(kslc.reshape(bs, S_PAD, 2, NSA_KV_HEADS, HEAD_DIM)[:, :s_new])
        outs["sb_s"].append(ksb.reshape(bs, S_PAD, 2, SB_HEADS, HEAD_DIM)[:, :s_new])

    y_s = xs.reshape(bs, S_PAD, D_MODEL)[:, :s_new]
    win_s = win_s.transpose(0, 1, 5, 2, 3, 4)
    return (xp, y_s,
            _rows_major(jnp.stack(outs["cmp_p"]), (depth, bp), NSA_KV_HEADS), jnp.stack(outs["cmp_s"]),
            _rows_major(jnp.stack(outs["slc_p"]), (depth, bp), NSA_KV_HEADS), jnp.stack(outs["slc_s"]),
            _rows_major(jnp.stack(outs["win_p"]), (depth, bp), NSA_KV_HEADS), win_s,
            _rows_major(jnp.stack(outs["sb_p"]), (depth, bp), SB_HEADS), jnp.stack(outs["sb_s"]))
```

```python
import functools

import numpy as np
import jax
import jax.numpy as jnp
from jax import lax
from jax.experimental import pallas as pl
from jax.experimental.pallas import tpu as pltpu

F32 = jnp.float32
BF16 = jnp.bfloat16

D_MODEL = 1024
HEAD_DIM = 64
NSA_HEADS = 8
NSA_KV_HEADS = 2
NSA_GROUP = NSA_HEADS // NSA_KV_HEADS
CMP_LEN = 32
CMP_STRIDE = 16
SEL_BLOCK = 64
SEL_TOP = 16
WINDOW = 512
SB_HEADS = 8
D_NSA = NSA_HEADS * HEAD_DIM
D_SB = SB_HEADS * HEAD_DIM
D_MIX = D_NSA + D_SB
D_FF = -(-8 * D_MODEL // (3 * 256)) * 256
KV_W = 2 * NSA_KV_HEADS * HEAD_DIM
PAGE_SIZE = 128
FORCE_BONUS = 1.0e4
NEG = -1.0e30
EPS = 1e-6
SLOPES = tuple(2.0 ** (-8.0 * (i + 1.0) / NSA_HEADS) for i in range(NSA_HEADS))

LANES = 128
S_PAD = 8
QB = 128
SEL_TILE = 512
SB_TQ = 256
SB_TK = 256
ROW_TILE = 256
VMEM_LIMIT = 56 * 1024 * 1024

C_QN = (0, 1024)
C_CMP = (1024, 1280)
C_SLC = (1280, 1536)
C_WIN = (1536, 1792)
C_GN = (1792, 1920)
C_QS = (1920, 2432)
C_SB = (2432, 3456)
C_GM = (3456, 4480)
W_CAT = 4480
R_CMP = (0, 256)
R_SLC = (256, 512)
R_WIN = (512, 768)
R_SB = (768, 1792)
W_KVT = 1792


def _dot(a, b):
    return jnp.dot(a, b, preferred_element_type=F32)


def _dot_nt(a, b):
    return lax.dot_general(a, b, (((1,), (1,)), ((), ())), preferred_element_type=F32)


def _split3(a):
    a1 = a.astype(BF16)
    r1 = a - a1.astype(F32)
    a2 = r1.astype(BF16)
    a3 = (r1 - a2.astype(F32)).astype(BF16)
    return a1, a2, a3


def _dot3_l(a, b):
    a1, a2, a3 = _split3(a)
    return _dot(a1, b) + _dot(a2, b) + _dot(a3, b)


def _dot3_r(a, b):
    b1, b2, b3 = _split3(b)
    return _dot(a, b1) + _dot(a, b2) + _dot(a, b3)


def _sigmoid(x):
    return 1.0 / (1.0 + jnp.exp(-x))


def _rms(x, g):
    return (x * lax.rsqrt(jnp.mean(x * x, axis=-1, keepdims=True) + EPS)) * g


def _masked_softmax(s, mask):
    s = jnp.where(mask, s, NEG)
    m = jnp.max(s, axis=-1, keepdims=True)
    p = jnp.where(mask, jnp.exp(s - m), 0.0)
    return p / jnp.maximum(jnp.sum(p, axis=-1, keepdims=True), 1e-30)


def _softmax_unnorm(s, mask):
    s = jnp.where(mask, s, 2.0 * NEG)
    m = jnp.maximum(jnp.max(s, axis=-1, keepdims=True), NEG)
    p = jnp.exp(s - m)
    return p, jnp.sum(p, axis=-1, keepdims=True)


def _topk_select(score, n_blk):
    n_pad = -(-n_blk // 8) * 8
    st = score.T[:n_pad, :]
    j = lax.broadcasted_iota(jnp.int32, (n_pad, LANES), 0)
    rank = jnp.zeros((n_pad, LANES), F32)
    for i in range(n_blk):
        row = jnp.broadcast_to(st[i:i + 1, :], (n_pad, LANES))
        beats = (row > st) | ((row == st) & (j > i))
        rank = rank + jnp.where(beats, 1.0, 0.0)
    sel_t = jnp.where(rank < float(SEL_TOP), 1.0, 0.0)
    if n_pad < LANES:
        sel_t = jnp.concatenate([sel_t, jnp.zeros((LANES - n_pad, LANES), F32)], axis=0)
    return sel_t.T


def _pair_heads(pieces, g):
    lane = lax.broadcasted_iota(jnp.int32, pieces[0].shape, 1)
    outs = []
    for m in range(2):
        a, b = pieces[2 * m], pieces[2 * m + 1]
        if g == 0:
            left, right = a, pltpu.roll(b, 64, 1)
        else:
            left, right = pltpu.roll(a, 64, 1), b
        outs.append(jnp.where(lane < 64, left, right))
    return jnp.concatenate(outs, axis=1)


def _pad_rows(x, rows):
    return jnp.concatenate([x, jnp.zeros((rows - x.shape[0], x.shape[1]), x.dtype)], axis=0)


def _inproj_sample_kernel(x_ref, g_ref, w_ref, qn_ref, cmp_ref, slc_ref, win_ref, gn_ref, qs_ref, sb_ref, gm_ref):
    hb = _rms(x_ref[...], g_ref[...]).astype(BF16)

    def proj(sec):
        return _dot(hb, w_ref[:, sec[0]:sec[1]])

    qn_ref[...] = (proj(C_QN) * (HEAD_DIM ** -0.5)).astype(BF16)
    cmp_ref[...] = proj(C_CMP)
    slc_ref[...] = proj(C_SLC)
    win_ref[...] = proj(C_WIN)
    gn_ref[...] = _sigmoid(proj(C_GN))
    qs_ref[...] = (proj(C_QS) * (HEAD_DIM ** -0.5)).astype(BF16)
    sb_ref[...] = proj(C_SB)
    gm_ref[...] = _sigmoid(proj(C_GM))


def _inproj_sample(x2d, g, w_cat):
    n = x2d.shape[0]
    tm = min(ROW_TILE, n)
    widths = [(1024, BF16), (256, F32), (256, F32), (256, F32), (128, F32), (512, BF16), (1024, F32), (1024, F32)]
    return pl.pallas_call(
        _inproj_sample_kernel,
        grid=(n // tm,),
        in_specs=[pl.BlockSpec((tm, D_MODEL), lambda i: (i, 0)),
                  pl.BlockSpec((1, D_MODEL), lambda i: (0, 0)),
                  pl.BlockSpec((D_MODEL, W_CAT), lambda i: (0, 0))],
        out_specs=[pl.BlockSpec((tm, w), lambda i: (i, 0)) for w, _ in widths],
        out_shape=[jax.ShapeDtypeStruct((n, w), dt) for w, dt in widths],
        compiler_params=pltpu.CompilerParams(dimension_semantics=("arbitrary",), vmem_limit_bytes=VMEM_LIMIT),
        name="inproj_sample",
    )(x2d, g, w_cat)


def _inproj_prompt_kernel(x_ref, g_ref, w_ref, wt_ref, qn_ref, gn_ref, qs_ref, gm_ref, cmp_ref,
                          cmpt_ref, slct_ref, wint_ref, sbt_ref, slctb_ref, wintb_ref, sbtb_ref):
    hb = _rms(x_ref[0], g_ref[...]).astype(BF16)

    def proj(sec):
        return _dot(hb, w_ref[:, sec[0]:sec[1]])

    def proj_t(sec):
        return _dot_nt(wt_ref[sec[0]:sec[1], :], hb)

    qn_ref[0] = (proj(C_QN) * (HEAD_DIM ** -0.5)).astype(BF16)
    gn_ref[0] = _sigmoid(proj(C_GN))
    qs_ref[0] = (proj(C_QS) * (HEAD_DIM ** -0.5)).astype(BF16)
    gm_ref[0] = _sigmoid(proj(C_GM))
    cmp_ref[0] = proj(C_CMP)
    cmpt_ref[0] = proj_t(R_CMP)
    u = proj_t(R_SLC)
    slct_ref[0] = u
    slctb_ref[0] = u.astype(BF16)
    u = proj_t(R_WIN)
    wint_ref[0] = u
    wintb_ref[0] = u.astype(BF16)
    u = proj_t(R_SB)
    sbt_ref[0] = u
    sbtb_ref[0] = u.astype(BF16)


def _inproj_prompt(x3d, g, w_cat, w_kvt):
    b, t, _ = x3d.shape
    tm = ROW_TILE
    row_outs = [(1024, BF16), (128, F32), (512, BF16), (1024, F32), (256, F32)]
    col_outs = [(256, F32), (256, F32), (256, F32), (1024, F32), (256, BF16), (256, BF16), (1024, BF16)]
    return pl.pallas_call(
        _inproj_prompt_kernel,
        grid=(b, t // tm),
        in_specs=[pl.BlockSpec((1, tm, D_MODEL), lambda i, j: (i, j, 0)),
                  pl.BlockSpec((1, D_MODEL), lambda i, j: (0, 0)),
                  pl.BlockSpec((D_MODEL, W_CAT), lambda i, j: (0, 0)),
                  pl.BlockSpec((W_KVT, D_MODEL), lambda i, j: (0, 0))],
        out_specs=([pl.BlockSpec((1, tm, w), lambda i, j: (i, j, 0)) for w, _ in row_outs]
                   + [pl.BlockSpec((1, w, tm), lambda i, j: (i, 0, j)) for w, _ in col_outs]),
        out_shape=([jax.ShapeDtypeStruct((b, t, w), dt) for w, dt in row_outs]
                   + [jax.ShapeDtypeStruct((b, w, t), dt) for w, dt in col_outs]),
        compiler_params=pltpu.CompilerParams(dimension_semantics=("arbitrary", "arbitrary"),
                                             vmem_limit_bytes=VMEM_LIMIT),
        name="inproj_prompt",
    )(x3d, g, w_cat, w_kvt)


def _compress_kernel(x_ref, pe_ref, wlo_ref, whi_ref, out_ref):
    x = x_ref[0]
    rows = x.shape[0]
    lo = _dot((x + pe_ref[0:1, :]).astype(BF16), wlo_ref[...])
    hi = _dot((x + pe_ref[1:2, :]).astype(BF16), whi_ref[...])
    out = lo + pltpu.roll(hi, rows - 1, 0)
    c = lax.broadcasted_iota(jnp.int32, out.shape, 0)
    out_ref[0] = jnp.where(c < rows - 1, out, 0.0)


def _compress_prompt(kv_cmp, pe2, wlo, whi):
    b, t, _ = kv_cmp.shape
    rows = t // CMP_STRIDE
    x = kv_cmp.reshape(b, rows, CMP_STRIDE * KV_W)
    return pl.pallas_call(
        _compress_kernel,
        grid=(b,),
        in_specs=[pl.BlockSpec((1, rows, CMP_STRIDE * KV_W), lambda i: (i, 0, 0)),
                  pl.BlockSpec((2, CMP_STRIDE * KV_W), lambda i: (0, 0)),
                  pl.BlockSpec((CMP_STRIDE * KV_W, KV_W), lambda i: (0, 0)),
                  pl.BlockSpec((CMP_STRIDE * KV_W, KV_W), lambda i: (0, 0))],
        out_specs=pl.BlockSpec((1, rows, KV_W), lambda i: (i, 0, 0)),
        out_shape=jax.ShapeDtypeStruct((b, rows, KV_W), F32),
        compiler_params=pltpu.CompilerParams(dimension_semantics=("arbitrary",), vmem_limit_bytes=VMEM_LIMIT),
        name="compress_prompt",
    )(x, pe2, wlo, whi)


def _compress_sample_kernel(*refs, n_pages, n_req):
    refs = refs[2:]
    page_refs = refs[:n_req * n_pages]
    pe_ref, wlo_ref, whi_ref, out_ref, x_scr = refs[n_req * n_pages:]
    for i, r in enumerate(page_refs):
        xt = r[0, 0].reshape(KV_W, PAGE_SIZE).T
        for half in range(KV_W // LANES):
            x_scr[half, i * PAGE_SIZE:(i + 1) * PAGE_SIZE, :] = xt[:, half * LANES:(half + 1) * LANES]
    rows_req = n_pages * PAGE_SIZE // CMP_STRIDE
    rows = n_req * rows_req
    lo = jnp.zeros((rows, KV_W), F32)
    hi = jnp.zeros((rows, KV_W), F32)
    for p in range(CMP_STRIDE):
        cols = slice(p * KV_W, (p + 1) * KV_W)
        xp = jnp.concatenate([x_scr[half, pl.ds(p, rows, stride=CMP_STRIDE), :] for half in range(KV_W // LANES)],
                             axis=1)
        lo = lo + _dot((xp + pe_ref[0:1, cols]).astype(BF16), wlo_ref[cols, :])
        hi = hi + _dot((xp + pe_ref[1:2, cols]).astype(BF16), whi_ref[cols, :])
    out = lo + pltpu.roll(hi, rows - 1, 0)
    c = lax.broadcasted_iota(jnp.int32, out.shape, 0)
    out = jnp.where((c & (rows_req - 1)) < rows_req - 1, out, 0.0)
    for j in range(n_req):
        out_ref[j] = out[j * rows_req:(j + 1) * rows_req]


def _nsa_prompt_kernel(q_ref, gate_ref, kc_ref, slct_ref, wint_ref, m_ref, o_ref, m_scr, l_scr, acc_scr, *, n_sel):
    qb = pl.program_id(1)
    q0 = qb * QB
    q_all = q_ref[0]
    gates = gate_ref[0]
    kc = kc_ref[0]
    n_cmp_pad = kc.shape[0]
    kc2 = kc[:, :LANES].astype(BF16)
    vc2 = kc[:, LANES:].astype(BF16)
    qpos = q0 + lax.broadcasted_iota(jnp.int32, (QB, 1), 0)
    c_end = lax.broadcasted_iota(jnp.int32, (1, n_cmp_pad), 1) * CMP_STRIDE + (CMP_LEN - 1)
    dist_c = (qpos - c_end).astype(F32)
    mask_c = dist_c >= 0.0
    blk = lax.broadcasted_iota(jnp.int32, (QB, LANES), 1)
    cur = qpos >> 6
    allowed = (blk * SEL_BLOCK <= qpos) & (blk < n_sel)
    forced = (blk == 0) | (blk == cur) | (blk == cur - 1)
    w_start = pl.multiple_of(jnp.maximum(q0 - WINDOW, 0), QB)
    n_win = WINDOW + QB
    n_tiles = (q0 + QB + SEL_TILE - 1) // SEL_TILE

    n_rows = NSA_GROUP * QB

    qms, sels, o_cs = [], [], []
    for g in range(NSA_KV_HEADS):
        heads = [g * NSA_GROUP + r for r in range(NSA_GROUP)]
        qm = jnp.concatenate([q_all[:, h * LANES:(h + 1) * LANES] for h in heads], axis=0)
        s = _dot_nt(qm, kc2)
        p_sum = jnp.zeros((QB, n_cmp_pad), F32)
        p_rows = []
        for r, h in enumerate(heads):
            p, l = _softmax_unnorm(s[r * QB:(r + 1) * QB] - SLOPES[h] * dist_c, mask_c)
            p = p * (1.0 / jnp.maximum(l, 1e-30))
            p_sum = p_sum + p
            p_rows.append(p.astype(BF16))
        o_cs.append(_dot(jnp.concatenate(p_rows, axis=0), vc2))
        imp = _dot3_l(p_sum, m_ref[...])
        score = jnp.where(allowed, imp + jnp.where(forced, FORCE_BONUS, 0.0), -FORCE_BONUS)
        score = jnp.where(blk < n_sel, score, -3.0 * FORCE_BONUS)
        sels.append(jnp.where(allowed, _topk_select(score, n_sel), 0.0).astype(BF16))
        qms.append(qm)

    m_scr[...] = jnp.full(m_scr.shape, NEG, F32)
    l_scr[...] = jnp.zeros(l_scr.shape, F32)
    acc_scr[...] = jnp.zeros(acc_scr.shape, F32)

    def sel_tile(kt, carry):
        k0 = pl.multiple_of(kt * SEL_TILE, SEL_TILE)
        k2t = slct_ref[0, 0:LANES, pl.ds(k0, SEL_TILE)]
        v2t = slct_ref[0, LANES:2 * LANES, pl.ds(k0, SEL_TILE)]
        kpos = k0 + lax.broadcasted_iota(jnp.int32, (1, SEL_TILE), 1)
        dist = (qpos - kpos).astype(F32)
        e_row = lax.broadcasted_iota(jnp.int32, (LANES, SEL_TILE), 0)
        e_col = lax.broadcasted_iota(jnp.int32, (LANES, SEL_TILE), 1)
        expand = jnp.where(e_row == kt * (SEL_TILE // SEL_BLOCK) + (e_col >> 6), 1.0, 0.0).astype(BF16)
        for g in range(NSA_KV_HEADS):
            s = _dot(qms[g], k2t)
            mask = (_dot(sels[g], expand) > 0.5) & (dist >= 0.0)
            p_rows = []
            for r in range(NSA_GROUP):
                rows = slice(g * n_rows + r * QB, g * n_rows + (r + 1) * QB)
                sr = jnp.where(mask, s[r * QB:(r + 1) * QB] - SLOPES[g * NSA_GROUP + r] * dist, 2.0 * NEG)
                m_old = m_scr[rows]
                m_new = jnp.maximum(m_old, jnp.max(sr, axis=-1, keepdims=True))
                alpha = jnp.exp(m_old - m_new)
                p = jnp.exp(sr - m_new)
                l_scr[rows] = alpha * l_scr[rows] + jnp.sum(p, axis=-1, keepdims=True)
                m_scr[rows] = m_new
                acc_scr[rows] = alpha * acc_scr[rows]
                p_rows.append(p.astype(BF16))
            acc_scr[g * n_rows:(g + 1) * n_rows] += _dot_nt(jnp.concatenate(p_rows, axis=0), v2t)
        return carry

    lax.fori_loop(0, n_tiles, sel_tile, 0)
    o_s_all = acc_scr[...] * (1.0 / jnp.maximum(l_scr[...], 1e-30))

    kwt = wint_ref[0, :, pl.ds(w_start, n_win)]
    kpos = w_start + lax.broadcasted_iota(jnp.int32, (1, n_win), 1)
    dist_i = qpos - kpos
    mask_w = (dist_i >= 0) & (dist_i < WINDOW)
    dist_w = dist_i.astype(F32)
    for g in range(NSA_KV_HEADS):
        heads = [g * NSA_GROUP + r for r in range(NSA_GROUP)]
        s = _dot(qms[g], kwt[:LANES])
        p_rows, l_rows = [], []
        for r, h in enumerate(heads):
            p, l = _softmax_unnorm(s[r * QB:(r + 1) * QB] - SLOPES[h] * dist_w, mask_w)
            p_rows.append(p.astype(BF16))
            l_rows.append(l)
        o_w = _dot_nt(jnp.concatenate(p_rows, axis=0), kwt[LANES:])
        o_w = o_w * (1.0 / jnp.maximum(jnp.concatenate(l_rows, axis=0), 1e-30))
        o_c = o_cs[g]
        o_s = o_s_all[g * n_rows:(g + 1) * n_rows]

        pieces = []
        for r, h in enumerate(heads):
            rows = slice(r * QB, (r + 1) * QB)
            pieces.append(gates[:, 3 * h:3 * h + 1] * o_c[rows] + gates[:, 3 * h + 1:3 * h + 2] * o_s[rows]
                          + gates[:, 3 * h + 2:3 * h + 3] * o_w[rows])
        o_ref[0, :, g * 256:(g + 1) * 256] = _pair_heads(pieces, g)


def _sel_weights(n_cmp_pad, n_cmp, n_sel):
    ratio = SEL_BLOCK // CMP_STRIDE
    m = np.zeros((n_cmp_pad, LANES), np.float32)
    for j in range(n_sel):
        for off in range(-1, ratio):
            c = ratio * j + off
            if 0 <= c < n_cmp:
                m[c, j] += 1.0 if off in (-1, ratio - 1) else 2.0
    return jnp.asarray(m, BF16)


def _nsa_prompt(qn, gn, kc, slctb, wintb):
    b, t, _ = qn.shape
    n_sel = -(-t // SEL_BLOCK)
    n_cmp_pad = kc.shape[1]
    m_mat = _sel_weights(n_cmp_pad, t // CMP_STRIDE - 1, n_sel)
    return pl.pallas_call(
        functools.partial(_nsa_prompt_kernel, n_sel=n_sel),
        grid=(b, t // QB),
        in_specs=[pl.BlockSpec((1, QB, NSA_HEADS * LANES), lambda i, j: (i, j, 0)),
                  pl.BlockSpec((1, QB, LANES), lambda i, j: (i, j, 0)),
                  pl.BlockSpec((1, n_cmp_pad, KV_W), lambda i, j: (i, 0, 0)),
                  pl.BlockSpec((1, KV_W, t), lambda i, j: (i, 0, 0)),
                  pl.BlockSpec((1, KV_W, t), lambda i, j: (i, 0, 0)),
                  pl.BlockSpec((n_cmp_pad, LANES), lambda i, j: (0, 0))],
        out_specs=pl.BlockSpec((1, QB, D_NSA), lambda i, j: (i, j, 0)),
        out_shape=jax.ShapeDtypeStruct((b, t, D_NSA), F32),
        scratch_shapes=[pltpu.VMEM((NSA_HEADS * QB, 1), F32), pltpu.VMEM((NSA_HEADS * QB, 1), F32),
                        pltpu.VMEM((NSA_HEADS * QB, LANES), F32)],
        compiler_params=pltpu.CompilerParams(dimension_semantics=("arbitrary", "arbitrary"),
                                             vmem_limit_bytes=VMEM_LIMIT),
        name="nsa_prompt",
    )(qn, gn, kc, slctb, wintb, m_mat)


def _softplus(z):
    neg_abs = lax.bitcast_convert_type(lax.bitcast_convert_type(z, jnp.int32) | jnp.int32(-2 ** 31), F32)
    return jnp.maximum(z, 0.0) + jnp.log(1.0 + jnp.exp(neg_abs))


def _suffix_sums(x, u2):
    hi = x.astype(BF16)
    lo = (x - hi.astype(F32)).astype(BF16)
    return _dot(jnp.concatenate([hi, lo], axis=1), u2)


def _sb_prompt_kernel(q_ref, kt_ref, vt_ref, u_ref, o_ref, c_scr, acc_scr, z_scr, a_scr):
    qb = pl.program_id(2)
    q2 = q_ref[0]
    lane = lax.broadcasted_iota(jnp.int32, q2.shape, 1)
    qm = jnp.concatenate([jnp.where((lane >> 6) == h, q2, jnp.zeros_like(q2)) for h in range(2)], axis=0)
    c_scr[...] = jnp.zeros(c_scr.shape, F32)
    acc_scr[...] = jnp.zeros(acc_scr.shape, F32)
    u2 = u_ref[...]

    def scores(kt):
        k0 = pl.multiple_of(kt * SB_TK, SB_TK)
        return _dot(qm, kt_ref[0, :, pl.ds(k0, SB_TK)])

    def add_values(kt):
        k0 = pl.multiple_of(kt * SB_TK, SB_TK)
        acc_scr[...] += _dot_nt(a_scr[...], vt_ref[0, :, pl.ds(k0, SB_TK)])

    def weights(z, mask):
        sp = _softplus(z)
        if mask is not None:
            sp = jnp.where(mask, sp, 0.0)
        a = jnp.exp((z - sp) - (c_scr[...] + _suffix_sums(sp, u2)))
        if mask is not None:
            a = jnp.where(mask, a, 0.0)
        c_scr[...] += jnp.sum(sp, axis=-1, keepdims=True)
        return a.astype(BF16)

    row = lax.broadcasted_iota(jnp.int32, (2 * SB_TQ, SB_TK), 0) & (SB_TQ - 1)
    col = lax.broadcasted_iota(jnp.int32, (2 * SB_TQ, SB_TK), 1)
    z_scr[...] = scores(jnp.maximum(qb - 1, 0))
    a_scr[...] = weights(scores(qb), col < row)

    def earlier(i, carry):
        kt = qb - 1 - i
        z = z_scr[...]
        add_values(kt + 1)
        z_scr[...] = scores(jnp.maximum(kt - 1, 0))
        a_scr[...] = weights(z, None)
        return carry

    lax.fori_loop(0, qb, earlier, 0)
    add_values(0)
    lane_o = lax.broadcasted_iota(jnp.int32, (SB_TQ, LANES), 1)
    o_ref[0] = jnp.where(lane_o < HEAD_DIM, acc_scr[0:SB_TQ], acc_scr[SB_TQ:])


def _later_matrix(n):
    u = np.tril(np.ones((n, n), np.float32), -1)
    return jnp.asarray(np.concatenate([u, u], axis=0), BF16)


def _sb_prompt(qs, sbtb):
    b, t, _ = qs.shape
    n_hp = SB_HEADS // 2
    return pl.pallas_call(
        _sb_prompt_kernel,
        grid=(b, n_hp, t // SB_TQ),
        in_specs=[pl.BlockSpec((1, SB_TQ, LANES), lambda i, h, j: (i, j, h)),
                  pl.BlockSpec((1, LANES, t), lambda i, h, j: (i, h, 0)),
                  pl.BlockSpec((1, LANES, t), lambda i, h, j: (i, n_hp + h, 0)),
                  pl.BlockSpec((2 * SB_TK, SB_TK), lambda i, h, j: (0, 0))],
        out_specs=pl.BlockSpec((1, SB_TQ, LANES), lambda i, h, j: (i, j, h)),
        out_shape=jax.ShapeDtypeStruct((b, t, D_SB), F32),
        scratch_shapes=[pltpu.VMEM((2 * SB_TQ, 1), F32), pltpu.VMEM((2 * SB_TQ, LANES), F32),
                        pltpu.VMEM((2 * SB_TQ, SB_TK), F32), pltpu.VMEM((2 * SB_TQ, SB_TK), BF16)],
        compiler_params=pltpu.CompilerParams(dimension_semantics=("arbitrary", "arbitrary", "arbitrary"),
                                             vmem_limit_bytes=VMEM_LIMIT),
        name="sb_prompt",
    )(qs, sbtb, sbtb, _later_matrix(SB_TK))


def _nsa_sample_kernel(*refs, n_pages, n_req, aliased, **static):
    refs = refs[2:]
    qn_ref, gate_ref, kc_ref = refs[:3]
    pages = refs[3:3 + n_req * n_pages]
    refs = refs[3 + n_req * n_pages:]
    slcnew_ref, wcache_ref, wnew_ref, m_ref, slope_ref, gsum_ref, expand_ref = refs[:7]
    refs = refs[7:]
    if aliased:
        refs = refs[1:]
    onsa_ref, wout_ref = refs
    for j in range(n_req):
        one = pl.ds(j, 1)
        _nsa_sample_request(qn_ref.at[one], gate_ref.at[one], kc_ref.at[one], pages[j * n_pages:(j + 1) * n_pages],
                            slcnew_ref.at[one], wcache_ref.at[:, one], wnew_ref.at[one], m_ref, slope_ref, gsum_ref,
                            expand_ref, onsa_ref.at[one], wout_ref.at[:, one], **static)


def _nsa_sample_request(qn_ref, gate_ref, kc_ref, slc_pages, slcnew_ref, wcache_ref, wnew_ref, m_ref, slope_ref,
                        gsum_ref, expand_ref, onsa_ref, wout_ref, *, past_len, s_new, n_sel):
    rows = NSA_HEADS * S_PAD
    row = lax.broadcasted_iota(jnp.int32, (rows, 1), 0)
    step = row & (S_PAD - 1)
    qpos = past_len + step
    slope = slope_ref[...]
    gates = gate_ref[0]
    q_all = qn_ref[0]
    qm = jnp.concatenate([q_all[:, h * LANES:(h + 1) * LANES] for h in range(NSA_HEADS)], axis=0)
    lane = lax.broadcasted_iota(jnp.int32, (rows, LANES), 1)

    kc = kc_ref[0]
    n_cmp_pad = kc.shape[0]
    c_end = lax.broadcasted_iota(jnp.int32, (1, n_cmp_pad), 1) * CMP_STRIDE + (CMP_LEN - 1)
    dist_c = (qpos - c_end).astype(F32)
    p_c = _masked_softmax(_dot_nt(qm, kc[:, :LANES].astype(BF16)) - slope * dist_c, dist_c >= 0.0)
    o_c = _dot(p_c.astype(BF16), kc[:, LANES:].astype(BF16))

    p_sum = _dot3_r(gsum_ref[...], p_c)
    imp = _dot3_l(p_sum, m_ref[...])
    cur = qpos >> 6
    allowed = (lane * SEL_BLOCK <= qpos) & (lane < n_sel)
    forced = (lane == 0) | (lane == cur) | (lane == cur - 1)
    score = jnp.where(allowed, imp + jnp.where(forced, FORCE_BONUS, 0.0), -FORCE_BONUS)
    score = jnp.where(lane < n_sel, score, -3.0 * FORCE_BONUS)
    sel = _topk_select(_pad_rows(score, LANES), n_sel)[:rows]
    sel = jnp.where(allowed, sel, 0.0)

    k2t = jnp.concatenate([r[0, 0, 0].reshape(LANES, PAGE_SIZE).astype(BF16) for r in slc_pages], axis=1)
    v2t = jnp.concatenate([r[0, 0, 1].reshape(LANES, PAGE_SIZE).astype(BF16) for r in slc_pages], axis=1)
    new = _pad_rows(slcnew_ref[0], LANES).astype(BF16)
    s_past = _dot(qm, k2t)
    s_nw = _dot_nt(qm, new[:, :LANES])
    kpos = lax.broadcasted_iota(jnp.int32, (1, past_len), 1)
    dist_p = (qpos - kpos).astype(F32)
    mask_p = (_dot(sel.astype(BF16), expand_ref[...]) > 0.5) & (dist_p >= 0.0)
    dist_n = (step - lane).astype(F32)
    last_blk = past_len // SEL_BLOCK
    sel_last = jnp.sum(jnp.where(lane == last_blk, sel, 0.0), axis=-1, keepdims=True)
    mask_n = (dist_n >= 0.0) & (sel_last > 0.5) & (lane < S_PAD)
    s_past = jnp.where(mask_p, s_past - slope * dist_p, NEG)
    s_nw = jnp.where(mask_n, s_nw - slope * dist_n, NEG)
    m = jnp.maximum(jnp.max(s_past, axis=-1, keepdims=True), jnp.max(s_nw, axis=-1, keepdims=True))
    p_past = jnp.where(mask_p, jnp.exp(s_past - m), 0.0)
    p_new = jnp.where(mask_n, jnp.exp(s_nw - m), 0.0)
    denom = jnp.sum(p_past, axis=-1, keepdims=True) + jnp.sum(p_new, axis=-1, keepdims=True)
    o_s = (_dot_nt(p_past.astype(BF16), v2t) + _dot(p_new.astype(BF16), new[:, LANES:])) / jnp.maximum(denom, 1e-30)

    wc = wcache_ref[0, 0]
    win_buf = wc.shape[-1]
    wct = wc.reshape(KV_W, win_buf)
    wnew = wnew_ref[0]
    wnew_pad = _pad_rows(wnew, LANES)
    shifted = pltpu.roll(wct, win_buf - s_new, 1)
    tail = pltpu.roll(wnew_pad.T, LANES - s_new, 1)
    lane_w = lax.broadcasted_iota(jnp.int32, (KV_W, LANES), 1)
    last = jnp.where(lane_w >= LANES - s_new, tail, shifted[:, win_buf - LANES:])
    wout_ref[0, 0] = jnp.concatenate([shifted[:, :win_buf - LANES], last], axis=1).reshape(wc.shape)
    wcb = wct.astype(BF16)
    wnb = wnew_pad.astype(BF16)
    kpos_w = (past_len - win_buf) + lax.broadcasted_iota(jnp.int32, (1, win_buf), 1)
    dist_wi = qpos - kpos_w
    mask_wp = (dist_wi >= 0) & (dist_wi < WINDOW)
    mask_wn = (dist_n >= 0.0) & (lane < S_PAD)
    s_wp = jnp.where(mask_wp, _dot(qm, wcb[:LANES]) - slope * dist_wi.astype(F32), NEG)
    s_wn = jnp.where(mask_wn, _dot_nt(qm, wnb[:, :LANES]) - slope * dist_n, NEG)
    m = jnp.maximum(jnp.max(s_wp, axis=-1, keepdims=True), jnp.max(s_wn, axis=-1, keepdims=True))
    p_wp = jnp.where(mask_wp, jnp.exp(s_wp - m), 0.0)
    p_wn = jnp.where(mask_wn, jnp.exp(s_wn - m), 0.0)
    denom = jnp.sum(p_wp, axis=-1, keepdims=True) + jnp.sum(p_wn, axis=-1, keepdims=True)
    o_w = (_dot_nt(p_wp.astype(BF16), wcb[LANES:]) + _dot(p_wn.astype(BF16), wnb[:, LANES:])) / jnp.maximum(denom, 1e-30)

    for g in range(NSA_KV_HEADS):
        pieces = []
        for r in range(NSA_GROUP):
            h = g * NSA_GROUP + r
            rs = slice(h * S_PAD, (h + 1) * S_PAD)
            pieces.append(gates[:, 3 * h:3 * h + 1] * o_c[rs] + gates[:, 3 * h + 1:3 * h + 2] * o_s[rs]
                          + gates[:, 3 * h + 2:3 * h + 3] * o_w[rs])
        onsa_ref[0, :, g * 256:(g + 1) * 256] = _pair_heads(pieces, g)


def _sb_sample_kernel(*refs, n_pages):
    refs = refs[2:]
    qs_ref = refs[0]
    sb_pages = refs[1:1 + n_pages]
    sbnew_ref, u_ref, osb_ref = refs[1 + n_pages:]

    rows = SB_HEADS * S_PAD
    step = lax.broadcasted_iota(jnp.int32, (rows, 1), 0) & (S_PAD - 1)
    lane = lax.broadcasted_iota(jnp.int32, (rows, LANES), 1)
    q_sb = qs_ref[0]
    q_rep = jnp.concatenate([q_sb] * SB_HEADS, axis=0)
    lane_sb = lax.broadcasted_iota(jnp.int32, q_rep.shape, 1)
    row_sb = lax.broadcasted_iota(jnp.int32, q_rep.shape, 0)
    q_bd = jnp.where((lane_sb >> 6) == (row_sb >> 3), q_rep, jnp.zeros_like(q_rep))
    kt = jnp.concatenate([r[0, 0, 0].reshape(D_SB, PAGE_SIZE).astype(BF16) for r in sb_pages], axis=1)
    vt = jnp.concatenate([r[0, 0, 1].reshape(D_SB, PAGE_SIZE).astype(BF16) for r in sb_pages], axis=1)
    sbn = _pad_rows(sbnew_ref[0], LANES).astype(BF16)
    z_past = _dot(q_bd, kt)
    z_new = _dot_nt(q_bd, sbn[:, :D_SB])
    u = u_ref[...]
    mask_sn = lane < step
    sp = jnp.where(mask_sn, _softplus(z_new), 0.0)
    a_new = jnp.where(mask_sn, jnp.exp((z_new - sp) - _suffix_sums(sp, u)), 0.0)
    carry = jnp.sum(sp, axis=-1, keepdims=True)
    a_tiles = [None] * n_pages
    for t in range(n_pages - 1, -1, -1):
        z = z_past[:, t * LANES:(t + 1) * LANES]
        sp = _softplus(z)
        a_tiles[t] = jnp.exp((z - sp) - (carry + _suffix_sums(sp, u))).astype(BF16)
        carry = carry + jnp.sum(sp, axis=-1, keepdims=True)
    o_all = _dot_nt(jnp.concatenate(a_tiles, axis=1), vt) + _dot(a_new.astype(BF16), sbn[:, D_SB:])
    lane_o = lax.broadcasted_iota(jnp.int32, (S_PAD, D_SB), 1)
    o_sb = jnp.zeros((S_PAD, D_SB), F32)
    for h in range(SB_HEADS):
        o_sb = o_sb + jnp.where((lane_o >> 6) == h, o_all[h * S_PAD:(h + 1) * S_PAD], 0.0)
    osb_ref[0] = o_sb


def _sample_kernel(*refs, n_pages, aliased, **static):
    pre, r = refs[:2], refs[2:]
    cmp_in, r = r[:n_pages + 3], r[n_pages + 3:]
    nsa_head, slc_pages, nsa_tail, r = r[:2], r[2:2 + n_pages], r[2 + n_pages:9 + n_pages], r[9 + n_pages:]
    sb_in, r = r[:n_pages + 3], r[n_pages + 3:]
    if aliased:
        r = r[1:]
    onsa_ref, wout_ref, osb_ref, kc_scr, x_scr = r
    _compress_sample_kernel(*pre, *cmp_in, kc_scr, x_scr, n_pages=n_pages, n_req=1)
    _nsa_sample_kernel(*pre, *nsa_head, kc_scr, *slc_pages, *nsa_tail, onsa_ref, wout_ref, n_pages=n_pages, n_req=1,
                       aliased=False, **static)
    _sb_sample_kernel(*pre, *sb_in, osb_ref, n_pages=n_pages)


def _sample_attend(page_table, lidx, cache_cmpt, pe2, wlo, whi, qn, gn, cache_slct, slc_new, cache_wint, win_new, qs,
                   cache_sbt, sb_new, win_buf_out, s_new):
    nb, n_pages = page_table.shape
    past_len = n_pages * PAGE_SIZE
    assert s_new < CMP_STRIDE and s_new <= S_PAD and past_len % SEL_BLOCK == 0
    n_sel = -(-(past_len + s_new) // SEL_BLOCK)
    n_cmp_pad = past_len // CMP_STRIDE
    depth, _, _, _, _, win_buf = cache_wint.shape
    rows = NSA_HEADS * S_PAD
    m_mat = _sel_weights(n_cmp_pad, (past_len + s_new) // CMP_STRIDE - 1, n_sel)
    slope = jnp.asarray(np.repeat(np.asarray(SLOPES, np.float32), S_PAD).reshape(rows, 1))
    rr = np.arange(rows)
    gsum = jnp.asarray(((rr[:, None] // (S_PAD * NSA_GROUP) == rr[None, :] // (S_PAD * NSA_GROUP))
                        & (rr[:, None] % S_PAD == rr[None, :] % S_PAD)).astype(np.float32), BF16)
    expand = jnp.asarray((np.arange(LANES)[:, None] == np.arange(past_len)[None, :] // SEL_BLOCK).astype(np.float32),
                         BF16)
    aliased = win_buf_out is not None
    nsa_dims = (2, NSA_KV_HEADS, HEAD_DIM)
    sb_dims = (2, SB_HEADS, HEAD_DIM)

    def const(shape):
        return pl.BlockSpec(shape, lambda b, pt, li: tuple(0 for _ in shape), pipeline_mode=pl.Buffered(1))

    def per_req(shape):
        return pl.BlockSpec((1,) + shape, lambda b, pt, li: (b,) + tuple(0 for _ in shape))

    def page_spec(i, kv_dims):
        return pl.BlockSpec((1, 1) + kv_dims + (PAGE_SIZE,), lambda b, pt, li: (li[0], pt[b, i], 0, 0, 0, 0))

    win_spec = pl.BlockSpec((1, 1) + nsa_dims + (win_buf,), lambda b, pt, li: (li[0], b, 0, 0, 0, 0))
    in_specs = ([page_spec(i, nsa_dims) for i in range(n_pages)]
                + [const((2, CMP_STRIDE * KV_W)), const((CMP_STRIDE * KV_W, KV_W)), const((CMP_STRIDE * KV_W, KV_W))]
                + [per_req((S_PAD, NSA_HEADS * LANES)), per_req((S_PAD, LANES))]
                + [page_spec(i, nsa_dims) for i in range(n_pages)]
                + [per_req((S_PAD, KV_W)), win_spec, per_req((S_PAD, KV_W)),
                   const((n_cmp_pad, LANES)), const((rows, 1)), const((rows, rows)), const((LANES, past_len))]
                + [per_req((S_PAD, D_SB))] + [page_spec(i, sb_dims) for i in range(n_pages)]
                + [per_req((S_PAD, 2 * D_SB)), const((2 * LANES, LANES))])
    args = ([cache_cmpt] * n_pages + [pe2, wlo, whi] + [qn, gn] + [cache_slct] * n_pages
            + [slc_new, cache_wint, win_new, m_mat, slope, gsum, expand]
            + [qs] + [cache_sbt] * n_pages + [sb_new, _later_matrix(LANES)])
    aliases = {}
    if aliased:
        in_specs.append(pl.BlockSpec(memory_space=pl.ANY))
        args.append(win_buf_out)
        aliases = {2 + len(args) - 1: 1}
    return pl.pallas_call(
        functools.partial(_sample_kernel, n_pages=n_pages, aliased=aliased, past_len=past_len, s_new=s_new,
                          n_sel=n_sel),
        grid_spec=pltpu.PrefetchScalarGridSpec(
            num_scalar_prefetch=2,
            grid=(nb,),
            in_specs=in_specs,
            out_specs=[per_req((S_PAD, D_NSA)), win_spec, per_req((S_PAD, D_SB))],
            scratch_shapes=[pltpu.VMEM((1, n_cmp_pad, KV_W), F32),
                            pltpu.VMEM((KV_W // LANES, past_len, LANES), F32)]),
        out_shape=[jax.ShapeDtypeStruct((nb, S_PAD, D_NSA), F32),
                   jax.ShapeDtypeStruct((depth, nb) + nsa_dims + (win_buf,), F32),
                   jax.ShapeDtypeStruct((nb, S_PAD, D_SB), F32)],
        input_output_aliases=aliases,
        compiler_params=pltpu.CompilerParams(dimension_semantics=("arbitrary",), vmem_limit_bytes=VMEM_LIMIT),
        name="sample_attend",
    )(page_table, lidx, *args)


def _post_kernel(x_ref, onsa_ref, osb_ref, gm_ref, wout_ref, gpost_ref, gpre2_ref, w1_ref, w2_ref, gpost2_ref, y_ref):
    gm = gm_ref[...]
    y = (_dot((gm[:, :D_NSA] * onsa_ref[...]).astype(BF16), wout_ref[0:D_NSA, :])
         + _dot((gm[:, D_NSA:] * osb_ref[...]).astype(BF16), wout_ref[D_NSA:, :]))
    x = x_ref[...] + _rms(y, gpost_ref[...])
    hb = _rms(x, gpre2_ref[...]).astype(BF16)
    a = _dot(hb, w1_ref[:, 0:D_FF])
    b = _dot(hb, w1_ref[:, D_FF:])
    act = (a * _sigmoid(a)) * b
    y_ref[...] = x + _rms(_dot(act.astype(BF16), w2_ref[...]), gpost2_ref[...])


def _post(x2d, onsa, osb, gm, wout, gpost, gpre2, w1, w2, gpost2):
    n = x2d.shape[0]
    tm = min(ROW_TILE, n)

    def rows(w):
        return pl.BlockSpec((tm, w), lambda i: (i, 0))

    def whole(shape):
        return pl.BlockSpec(shape, lambda i: (0, 0), pipeline_mode=pl.Buffered(1))

    return pl.pallas_call(
        _post_kernel,
        grid=(n // tm,),
        in_specs=[rows(D_MODEL), rows(D_NSA), rows(D_SB), rows(D_MIX), whole((D_MIX, D_MODEL)), whole((1, D_MODEL)),
                  whole((1, D_MODEL)), whole((D_MODEL, 2 * D_FF)), whole((D_FF, D_MODEL)), whole((1, D_MODEL))],
        out_specs=rows(D_MODEL),
        out_shape=jax.ShapeDtypeStruct((n, D_MODEL), F32),
        compiler_params=pltpu.CompilerParams(dimension_semantics=("arbitrary",), vmem_limit_bytes=VMEM_LIMIT),
        name="post",
    )(x2d, onsa, osb, gm, wout, gpost, gpre2, w1, w2, gpost2)


def _relayout_w_in(w):
    z64 = jnp.zeros((D_MODEL, HEAD_DIM), w.dtype)
    cols = []
    for h in range(NSA_HEADS):
        wh = w[:, h * HEAD_DIM:(h + 1) * HEAD_DIM]
        cols += [wh, z64] if h // NSA_GROUP == 0 else [z64, wh]
    off = D_NSA
    cols.append(w[:, off:off + 3 * KV_W])
    off += 3 * KV_W
    cols.append(w[:, off:off + 3 * NSA_HEADS])
    cols.append(jnp.zeros((D_MODEL, LANES - 3 * NSA_HEADS), w.dtype))
    off += 3 * NSA_HEADS
    cols.append(w[:, off:])
    w_cat = jnp.concatenate(cols, axis=1).astype(BF16)
    w_kvt = jnp.concatenate([w_cat[:, C_CMP[0]:C_WIN[1]], w_cat[:, C_SB[0]:C_SB[1]]], axis=1).T
    return w_cat, w_kvt


def _relayout_cmp(pe, w):
    eye_k = jnp.eye(2, dtype=w.dtype)
    eye_g = jnp.eye(NSA_KV_HEADS, dtype=w.dtype)
    outs = []
    for half in range(2):
        wh = w[:, half * CMP_STRIDE:(half + 1) * CMP_STRIDE]
        big = jnp.einsum('kpde,kK,gG->pkgdKGe', wh, eye_k, eye_g)
        outs.append(big.reshape(CMP_STRIDE * KV_W, KV_W).astype(BF16))
    pes = []
    for half in range(2):
        ph = pe[:, half * CMP_STRIDE:(half + 1) * CMP_STRIDE]
        ph = jnp.broadcast_to(ph.transpose(1, 0, 2)[:, :, None, :], (CMP_STRIDE, 2, NSA_KV_HEADS, HEAD_DIM))
        pes.append(ph.reshape(1, CMP_STRIDE * KV_W))
    return jnp.concatenate(pes, axis=0), outs[0], outs[1]


def _rows_major(xt, lead, n_heads):
    t = xt.shape[-1]
    k = len(lead)
    xt = xt.reshape(lead + (2, n_heads, HEAD_DIM, t))
    return xt.transpose(tuple(range(k)) + (k + 3, k, k + 1, k + 2))


def kernel(x_prompt, x_sample, cache_cmp_kv, cache_slc_kv, cache_win_kv, cache_sb_kv, page_table, g_pre_mix,
           g_post_mix, g_pre_ffn, g_post_ffn, w_in, pe_cmp, w_cmp, w_out, w_ffn_in, w_ffn_out):
    depth = w_in.shape[0]
    bp, t, _ = x_prompt.shape
    bs, s_new, _ = x_sample.shape
    c_cmpt = cache_cmp_kv.transpose(0, 1, 3, 4, 5, 2)
    c_slct = cache_slc_kv.transpose(0, 1, 3, 4, 5, 2)
    c_wint = cache_win_kv.transpose(0, 1, 3, 4, 5, 2)
    c_sbt = cache_sb_kv.transpose(0, 1, 3, 4, 5, 2)

    xp = x_prompt
    xs = jnp.pad(x_sample, ((0, 0), (0, S_PAD - s_new), (0, 0))).reshape(bs * S_PAD, D_MODEL)
    outs = {k: [] for k in ("cmp_p", "cmp_s", "slc_p", "slc_s", "win_p", "sb_p", "sb_s")}
    win_s = None
    for l in range(depth):
        w_cat, w_kvt = _relayout_w_in(w_in[l])
        pe2, wlo, whi = _relayout_cmp(pe_cmp[l], w_cmp[l])
        wout_b = w_out[l].astype(BF16)
        w1_b = w_ffn_in[l].astype(BF16)
        w2_b = w_ffn_out[l].astype(BF16)
        g1 = g_pre_mix[l].reshape(1, D_MODEL)
        g2 = g_post_mix[l].reshape(1, D_MODEL)
        g3 = g_pre_ffn[l].reshape(1, D_MODEL)
        g4 = g_post_ffn[l].reshape(1, D_MODEL)
        lidx = jnp.full((1,), l, jnp.int32)

        qn, gn, qs, gm, kcmp, cmpt, slct, wint, sbt, slctb, wintb, sbtb = _inproj_prompt(xp, g1, w_cat, w_kvt)
        kc = _compress_prompt(kcmp, pe2, wlo, whi)
        o_nsa = _nsa_prompt(qn, gn, kc, slctb, wintb)
        o_sb = _sb_prompt(qs, sbtb)
        xp = _post(xp.reshape(bp * t, D_MODEL), o_nsa.reshape(bp * t, D_NSA), o_sb.reshape(bp * t, D_SB),
                   gm.reshape(bp * t, D_MIX), wout_b, g2, g3, w1_b, w2_b, g4).reshape(bp, t, D_MODEL)
        outs["cmp_p"].append(cmpt)
        outs["slc_p"].append(slct)
        outs["win_p"].append(wint[:, :, t - min(WINDOW, t):])
        outs["sb_p"].append(sbt)

        qn, kcmp, kslc, kwin, gn, qs, ksb, gm = _inproj_sample(xs, g1, w_cat)
        o_nsa, win_s, o_sb = _sample_attend(
            page_table, lidx, c_cmpt, pe2, wlo, whi, qn.reshape(bs, S_PAD, -1), gn.reshape(bs, S_PAD, -1), c_slct,
            kslc.reshape(bs, S_PAD, KV_W), c_wint, kwin.reshape(bs, S_PAD, KV_W), qs.reshape(bs, S_PAD, D_SB), c_sbt,
            ksb.reshape(bs, S_PAD, 2 * D_SB), win_s, s_new)
        xs = _post(xs, o_nsa.reshape(bs * S_PAD, D_NSA), o_sb.reshape(bs * S_PAD, D_SB), gm, wout_b, g2, g3, w1_b,
                   w2_b, g4)
        outs["cmp_s"].append(kcmp.reshape(bs, S_PAD, 2, NSA_KV_HEADS, HEAD_DIM)[:, :s_new])
        outs["slc_s"].append(kslc.reshape(bs, S_PAD, 2, NSA_KV_HEADS, HEAD_DIM)[:, :s_new])
        outs["sb_s"].append(ksb.reshape(bs, S_PAD, 2, SB_HEADS, HEAD_DIM)[:, :s_new])

    y_s = xs.reshape(bs, S_PAD, D_MODEL)[:, :s_new]
    win_s = win_s.transpose(0, 1, 5, 2, 3, 4)
    return (xp, y_s,
            _rows_major(jnp.stack(outs["cmp_p"]), (depth, bp), NSA_KV_HEADS), jnp.stack(outs["cmp_s"]),
            _rows_major(jnp.stack(outs["slc_p"]), (depth, bp), NSA_KV_HEADS), jnp.stack(outs["slc_s"]),
            _rows_major(jnp.stack(outs["win_p"]), (depth, bp), NSA_KV_HEADS), win_s,
            _rows_major(jnp.stack(outs["sb_p"]), (depth, bp), SB_HEADS), jnp.stack(outs["sb_s"]))
```
